```python
import math
import functools
import jax
import jax.numpy as jnp
from jax import lax
import numpy as np

D_MODEL = 1024
BATCH = 4
SEQ = 4096
DEPTH = 2
DEC_BATCH = 32
DEC_SEQ = 1
PAST_LEN = 8192
PAGE_SIZE = 128

N_AH = 8
HD = 64
VD = 2 * HD
ATTN_W = N_AH * VD
Q_BLOCK = 128
LRU_W = 1024
N_LB = 8
LB = LRU_W // N_LB
CONV_W = 4
LRU_C = 8.0
N_HH = 8
HK = 128
HV = 128
HGRN_W = N_HH * HV
H_CHUNK = 64
P_HEADS = 8
N_KEYS = 128
N_EXP = N_KEYS * N_KEYS
D_KEY = 128
P_TOPK = 16
P_BLOCK = 256
EPS = 1e-6

IN_SIZES = (N_AH * 2 * HD, N_AH * 2 * HD, ATTN_W, LRU_W, LRU_W, N_HH * HK, N_HH * HK, HGRN_W, HGRN_W, 3 * D_MODEL)
IN_W = sum(IN_SIZES)

kernel_name = 'hybrid_diffattn_rglru_hgrn2_peer_step'


def rms_norm(x, g):
    xf = x.astype(jnp.float32)
    y = xf * lax.rsqrt(jnp.mean(xf * xf, axis=-1, keepdims=True) + EPS)
    return (y * g.astype(jnp.float32)).astype(x.dtype)


def alibi_slopes():
    return jnp.exp2(-8.0 * jnp.arange(1, N_AH + 1, dtype=jnp.float32) / N_AH)


def diff_logits(q, k, q_pos, k_pos, slopes):
    s = jnp.einsum('bqhcd,bkhcd->bhcqk', q, k).astype(jnp.float32) * (HD ** -0.5)
    dist = (q_pos[:, None] - k_pos[None, :]).astype(jnp.float32)
    s = s - slopes[None, :, None, None, None] * dist
    return jnp.where(dist >= 0, s, -jnp.inf)


def diff_weights(logits, lam, dtype):
    p = jax.nn.softmax(logits, axis=-1)
    return (p[:, :, 0] - lam * p[:, :, 1]).astype(dtype)


def attn_prompt(q, k, v, lam, slopes):
    B, S = q.shape[0], q.shape[1]
    nb = S // Q_BLOCK
    k_pos = jnp.arange(S, dtype=jnp.int32)
    qb = jnp.moveaxis(q.reshape(B, nb, Q_BLOCK, N_AH, 2, HD), 1, 0)

    def one(args):
        qi, i = args
        q_pos = i * Q_BLOCK + jnp.arange(Q_BLOCK, dtype=jnp.int32)
        w = diff_weights(diff_logits(qi, k, q_pos, k_pos, slopes), lam, v.dtype)
        return jnp.einsum('bhqk,bkhe->bqhe', w, v)

    o = lax.map(one, (qb, jnp.arange(nb, dtype=jnp.int32)))
    return jnp.moveaxis(o, 0, 1).reshape(B, S, N_AH, VD)


def attn_sample(q, k, v, lam, k_cache, v_cache, page_table, slopes):
    Bd, T = q.shape[0], q.shape[1]
    past = page_table.shape[1] * PAGE_SIZE
    k_past = k_cache[page_table].reshape(Bd, past, N_AH, 2, HD)
    v_past = v_cache[page_table].reshape(Bd, past, N_AH, VD)
    q_pos = past + jnp.arange(T, dtype=jnp.int32)
    s_past = diff_logits(q, k_past, q_pos, jnp.arange(past, dtype=jnp.int32), slopes)
    s_new = diff_logits(q, k, q_pos, q_pos, slopes)
    w = diff_weights(jnp.concatenate([s_past, s_new], axis=-1), lam, v.dtype)
    return (jnp.einsum('bhqk,bkhe->bqhe', w[..., :past], v_past)
            + jnp.einsum('bhqk,bkhe->bqhe', w[..., past:], v))


def causal_conv(x, buf, w, b):
    T = x.shape[1]
    xp = jnp.concatenate([buf.astype(x.dtype), x], axis=1)
    y = b + xp[:, 0:T] * w[0]
    for j in range(1, CONV_W):
        y = y + xp[:, j:j + T] * w[j]
    return y, xp[:, T:]


def _lin_comb(left, right):
    a1, b1 = left
    a2, b2 = right
    return a1 * a2, a2 * b1 + b2


def rg_lru(x, h0, wa, ba, wx, bx, lam):
    B, T, W = x.shape
    xb = x.reshape(B, T, N_LB, LB)
    r = jax.nn.sigmoid(jnp.einsum('btnl,nlm->btnm', xb, wa).reshape(B, T, W) + ba).astype(jnp.float32)
    i = jax.nn.sigmoid(jnp.einsum('btnl,nlm->btnm', xb, wx).reshape(B, T, W) + bx)
    log_a = -LRU_C * r * jax.nn.softplus(-lam.astype(jnp.float32))
    a = jnp.exp(log_a)
    u = jnp.sqrt(-jnp.expm1(2.0 * log_a)) * (i * x).astype(jnp.float32)
    u = u.at[:, 0].add(a[:, 0] * h0.astype(jnp.float32))
    _, h = lax.associative_scan(_lin_comb, (a, u), axis=1)
    return h.astype(x.dtype), h[:, -1].astype(x.dtype)


def hgrn2_chunked(q, log_f, v, s0):
    B, T, H, _ = q.shape
    C = min(H_CHUNK, T)
    n = -(-T // C)
    pad = n * C - T
    k = -jnp.expm1(log_f)

    def chunks(a):
        a = jnp.pad(a.astype(jnp.float32), ((0, 0), (0, pad), (0, 0), (0, 0)))
        return jnp.moveaxis(a.reshape(B, n, C, H, a.shape[-1]), 1, 0)

    tri = jnp.tril(jnp.ones((C, C), dtype=bool))[None, :, :, None, None]

    def step(S, inp):
        qc, kc, gc, vc = inp
        b = jnp.cumsum(gc, axis=1)
        o_inter = jnp.einsum('bthk,bhkv->bthv', qc * jnp.exp(b), S)
        decay = jnp.exp(jnp.where(tri, b[:, :, None] - b[:, None, :], -jnp.inf))
        att = jnp.einsum('bthk,btshk->btsh', qc, decay * kc[:, None])
        o_intra = jnp.einsum('btsh,bshv->bthv', att, vc)
        b_last = b[:, -1]
        S = S * jnp.exp(b_last)[..., None] + jnp.einsum('bshk,bshv->bhkv', kc * jnp.exp(b_last[:, None] - b), vc)
        return S, o_inter + o_intra

    s_fin, o = lax.scan(step, s0.astype(jnp.float32), (chunks(q), chunks(k), chunks(log_f), chunks(v)))
    o = jnp.moveaxis(o, 0, 1).reshape(B, n * C, H, v.shape[-1])[:, :T]
    return o, s_fin


def peer_ffn(x, wq, keys, u, v):
    N = x.shape[0]
    blk = min(P_BLOCK, N)
    nb = -(-N // blk)
    xb = jnp.pad(x, ((0, nb * blk - N), (0, 0))).reshape(nb, blk, D_MODEL)

    def one(xc):
        q = (xc @ wq).reshape(blk, P_HEADS, 2, D_KEY // 2)
        s = jnp.einsum('bhcd,hcnd->bhcn', q, keys).astype(jnp.float32)
        v1, i1 = lax.top_k(s[:, :, 0], P_TOPK)
        v2, i2 = lax.top_k(s[:, :, 1], P_TOPK)
        cand = (v1[..., :, None] + v2[..., None, :]).reshape(blk, P_HEADS, P_TOPK * P_TOPK)
        sc, pos = lax.top_k(cand, P_TOPK)
        e = (jnp.take_along_axis(i1, pos // P_TOPK, axis=-1) * N_KEYS
             + jnp.take_along_axis(i2, pos % P_TOPK, axis=-1))
        g = jax.nn.softmax(sc, axis=-1)
        act = jax.nn.gelu(jnp.einsum('bhkd,bd->bhk', u[e], xc).astype(jnp.float32), approximate=False)
        return jnp.einsum('bhk,bhkd->bd', (g * act).astype(x.dtype), v[e])

    return lax.map(one, xb).reshape(nb * blk, D_MODEL)[:N]


def layer(x, l, prm, lb_l, attn_fn, conv_buf, h0, s0):
    B, T, _ = x.shape
    xn = rms_norm(x, prm['norm1_g'][l])
    offs = np.cumsum(IN_SIZES)[:-1].tolist()
    aq, ak, av, lx, lg, hq, hf, hi, hg, gt = jnp.split(xn @ prm['w_in'][l], offs, axis=-1)

    qk_g = prm['qk_norm_g'][l]
    q = rms_norm(aq.reshape(B, T, N_AH, 2, HD), qk_g[0])
    k = rms_norm(ak.reshape(B, T, N_AH, 2, HD), qk_g[1])
    v = av.reshape(B, T, N_AH, VD)
    lv = prm['diff_lambda'][l].astype(jnp.float32)
    lam_init = 0.8 - 0.6 * math.exp(-0.3 * l)
    lam = jnp.exp(jnp.sum(lv[0] * lv[1])) - jnp.exp(jnp.sum(lv[2] * lv[3])) + lam_init
    o = attn_fn(q, k, v, lam)
    y_attn = (rms_norm(o, prm['subln_g'][l]) * (1.0 - lam_init)).reshape(B, T, ATTN_W)

    xc, conv_new = causal_conv(lx, conv_buf, prm['conv_w'][l], prm['conv_b'][l])
    hseq, h_last = rg_lru(xc, h0, prm['rg_wa'][l], prm['rg_ba'][l], prm['rg_wx'][l], prm['rg_bx'][l], prm['rg_lambda'][l])
    y_lru = hseq * jax.nn.gelu(lg, approximate=False)

    lb = lb_l.reshape(N_HH, HK)
    z = hf.reshape(B, T, N_HH, HK).astype(jnp.float32)
    log_f = jnp.logaddexp(jnp.log(lb), jnp.log1p(-lb) + jax.nn.log_sigmoid(z))
    qh = jax.nn.silu(hq.reshape(B, T, N_HH, HK))
    oh, s_fin = hgrn2_chunked(qh, log_f, hi.reshape(B, T, N_HH, HV), s0)
    y_h = (rms_norm(oh.astype(x.dtype), prm['hgrn_norm_g'][l])
           * jax.nn.silu(hg.reshape(B, T, N_HH, HV))).reshape(B, T, HGRN_W)

    ga, gl, gh = jnp.split(jax.nn.sigmoid(gt), 3, axis=-1)
    m = (ga * (y_attn @ prm['w_br_attn'][l]) + gl * (y_lru @ prm['w_br_lru'][l])
         + gh * (y_h @ prm['w_br_hgrn'][l]))
    x = x + m @ prm['w_out'][l]

    xn2 = rms_norm(x, prm['norm2_g'][l]).reshape(B * T, D_MODEL)
    x = x + peer_ffn(xn2, prm['peer_wq'][l], prm['peer_keys'][l], prm['peer_u'][l], prm['peer_v'][l]).reshape(B, T, D_MODEL)
    return x, (k.reshape(B, T, N_AH, 2 * HD), v, conv_new, h_last, s_fin.astype(x.dtype))


def setup_inputs(seed: int = 0) -> dict:
    key = jax.random.key(seed)
    ks = iter(jax.random.split(key, 40))

    def nrm(shape, scale):
        return scale * jax.random.normal(next(ks), shape, jnp.float32)

    n_pages = PAST_LEN // PAGE_SIZE
    n_used = DEC_BATCH * n_pages
    n_pool = n_used + (n_used + 3) // 4
    x_prompt = nrm((BATCH, SEQ, D_MODEL), 1.0)
    x_sample = nrm((DEC_BATCH, DEC_SEQ, D_MODEL), 1.0)
    cache_k = nrm((DEPTH, n_pool, PAGE_SIZE, N_AH, 2 * HD), 1.0)
    cache_v = nrm((DEPTH, n_pool, PAGE_SIZE, N_AH, VD), 1.0)
    state_conv = nrm((DEPTH, DEC_BATCH, CONV_W - 1, LRU_W), 1.0)
    state_lru = nrm((DEPTH, DEC_BATCH, LRU_W), 0.5)
    state_hgrn = nrm((DEPTH, DEC_BATCH, N_HH, HK, HV), 0.3)
    page_table = jax.random.permutation(next(ks), n_pool)[:n_used].reshape(DEC_BATCH, n_pages).astype(jnp.int32)
    a0 = jax.random.uniform(next(ks), (DEPTH, LRU_W), jnp.float32, 0.9, 0.999)
    a_base = a0 ** (1.0 / LRU_C)
    rg_lambda = jnp.log(a_base) - jnp.log1p(-a_base)
    return {
        'x_prompt': x_prompt,
        'x_sample': x_sample,
        'cache_k': cache_k,
        'cache_v': cache_v,
        'state_conv': state_conv,
        'state_lru': state_lru,
        'state_hgrn': state_hgrn,
        'page_table': page_table,
        'norm1_g': 1.0 + nrm((DEPTH, D_MODEL), 0.02),
        'norm2_g': 1.0 + nrm((DEPTH, D_MODEL), 0.02),
        'w_in': nrm((DEPTH, D_MODEL, IN_W), D_MODEL ** -0.5),
        'qk_norm_g': 1.0 + nrm((DEPTH, 2, HD), 0.02),
        'diff_lambda': nrm((DEPTH, 4, HD), 0.1),
        'subln_g': 1.0 + nrm((DEPTH, VD), 0.02),
        'conv_w': nrm((DEPTH, CONV_W, LRU_W), 0.5),
        'conv_b': nrm((DEPTH, LRU_W), 0.01),
        'rg_wa': nrm((DEPTH, N_LB, LB, LB), LB ** -0.5),
        'rg_ba': nrm((DEPTH, LRU_W), 0.01),
        'rg_wx': nrm((DEPTH, N_LB, LB, LB), LB ** -0.5),
        'rg_bx': nrm((DEPTH, LRU_W), 0.01),
        'rg_lambda': rg_lambda,
        'hgrn_lb': nrm((DEPTH, N_HH * HK), 1.0),
        'hgrn_norm_g': 1.0 + nrm((DEPTH, HV), 0.02),
        'w_br_attn': nrm((DEPTH, ATTN_W, D_MODEL), ATTN_W ** -0.5),
        'w_br_lru': nrm((DEPTH, LRU_W, D_MODEL), LRU_W ** -0.5),
        'w_br_hgrn': nrm((DEPTH, HGRN_W, D_MODEL), HGRN_W ** -0.5),
        'w_out': nrm((DEPTH, D_MODEL, D_MODEL), 0.5 * D_MODEL ** -0.5),
        'peer_wq': nrm((DEPTH, D_MODEL, P_HEADS * D_KEY), D_MODEL ** -0.5),
        'peer_keys': nrm((DEPTH, P_HEADS, 2, N_KEYS, D_KEY // 2), (D_KEY // 2) ** -0.5),
        'peer_u': nrm((DEPTH, N_EXP, D_MODEL), D_MODEL ** -0.5),
        'peer_v': nrm((DEPTH, N_EXP, D_MODEL), 0.1),
    }


def reference(x_prompt, x_sample, cache_k, cache_v, state_conv, state_lru, state_hgrn, page_table,
              norm1_g, norm2_g, w_in, qk_norm_g, diff_lambda, subln_g, conv_w, conv_b,
              rg_wa, rg_ba, rg_wx, rg_bx, rg_lambda, hgrn_lb, hgrn_norm_g,
              w_br_attn, w_br_lru, w_br_hgrn, w_out, peer_wq, peer_keys, peer_u, peer_v):
    prm = {
        'norm1_g': norm1_g, 'norm2_g': norm2_g, 'w_in': w_in, 'qk_norm_g': qk_norm_g,
        'diff_lambda': diff_lambda, 'subln_g': subln_g, 'conv_w': conv_w, 'conv_b': conv_b,
        'rg_wa': rg_wa, 'rg_ba': rg_ba, 'rg_wx': rg_wx, 'rg_bx': rg_bx, 'rg_lambda': rg_lambda,
        'hgrn_norm_g': hgrn_norm_g, 'w_br_attn': w_br_attn, 'w_br_lru': w_br_lru,
        'w_br_hgrn': w_br_hgrn, 'w_out': w_out, 'peer_wq': peer_wq, 'peer_keys': peer_keys,
        'peer_u': peer_u, 'peer_v': peer_v,
    }
    slopes = alibi_slopes()
    lb_cum = jnp.cumsum(jax.nn.softmax(hgrn_lb.astype(jnp.float32), axis=0), axis=0)
    lb = lb_cum - lb_cum[0]
    bp = x_prompt.shape[0]
    xp, xs = x_prompt, x_sample
    st_p, st_s = [], []
    for l in range(DEPTH):
        attn_p = functools.partial(attn_prompt, slopes=slopes)
        attn_s = functools.partial(attn_sample, k_cache=cache_k[l], v_cache=cache_v[l],
                                   page_table=page_table, slopes=slopes)
        xp, sp = layer(xp, l, prm, lb[l], attn_p,
                       jnp.zeros((bp, CONV_W - 1, LRU_W), xp.dtype),
                       jnp.zeros((bp, LRU_W), xp.dtype),
                       jnp.zeros((bp, N_HH, HK, HV), xp.dtype))
        xs, ss = layer(xs, l, prm, lb[l], attn_s, state_conv[l], state_lru[l], state_hgrn[l])
        st_p.append(sp)
        st_s.append(ss)
    k_p, v_p, conv_p, lru_p, hgrn_p = [jnp.stack(a) for a in zip(*st_p)]
    k_s, v_s, conv_s, lru_s, hgrn_s = [jnp.stack(a) for a in zip(*st_s)]
    return (xp, xs, k_p, v_p, conv_p, lru_p, hgrn_p, k_s, v_s, conv_s, lru_s, hgrn_s)
```

```python
import functools
import math

import jax
import jax.numpy as jnp
from jax import lax
from jax.experimental import pallas as pl
from jax.experimental.pallas import tpu as pltpu

F32 = jnp.float32
BF16 = jnp.bfloat16

D_MODEL = 1024
DEPTH = 2
PAGE_SIZE = 128
N_AH = 8
HD = 64
VD = 2 * HD
ATTN_W = N_AH * VD
LRU_W = 1024
N_LB = 8
LB = LRU_W // N_LB
CONV_W = 4
LRU_C = 8.0
N_HH = 8
HK = 128
HV = 128
HGRN_W = N_HH * HV
P_HEADS = 8
N_KEYS = 128
N_EXP = N_KEYS * N_KEYS
D_KEY = 128
P_TOPK = 16
EPS = 1e-6
IN_W = 12 * D_MODEL

COL_Q, COL_K, COL_V, COL_LX, COL_LG, COL_HQ, COL_HF, COL_HI, COL_HG, COL_GT = 0, 1, 2, 3, 4, 5, 6, 7, 8, 9

LANES = 128
SUBLANES = 8
VMEM_LIMIT = 56 * 1024 * 1024

NEG = -1e30
SQRT_HALF = 0.7071067811865476


def _cparams(sem):
    return pltpu.CompilerParams(dimension_semantics=sem, vmem_limit_bytes=VMEM_LIMIT)


def _split(a):
    hi = a.astype(BF16)
    lo = (a - hi.astype(F32)).astype(BF16)
    return hi, lo


def _dot(a, b):
    return jnp.dot(a, b, preferred_element_type=F32)


def _dot_nt(a, b):
    return lax.dot_general(a, b, (((1,), (1,)), ((), ())), preferred_element_type=F32)


def _dot3(a, b):
    ah, al = _split(a)
    bh, bl = _split(b)
    return _dot(ah, bh) + _dot(al, bh) + _dot(ah, bl)


def _dot3_nt(a, b):
    ah, al = _split(a)
    bh, bl = _split(b)
    return _dot_nt(ah, bh) + _dot_nt(al, bh) + _dot_nt(ah, bl)


def _dot2_exact_rhs(a, b_bf16):
    ah, al = _split(a)
    return _dot(ah, b_bf16) + _dot(al, b_bf16)


def _sigmoid(x):
    return 1.0 / (1.0 + jnp.exp(-x))


def _gelu(x):
    return 0.5 * x * (1.0 + lax.erf(x * SQRT_HALF))


def _silu(x):
    return x * _sigmoid(x)


def _softplus(x):
    return jnp.maximum(x, 0.0) + jnp.log1p(jnp.exp(-jnp.abs(x)))


def _expm1(x):
    u = jnp.exp(x)
    um1 = u - 1.0
    corrected = um1 * x / jnp.log(u)
    return jnp.where(um1 == 0.0, x, jnp.where(um1 == -1.0, -1.0, corrected))


def _rms(x, g):
    ms = jnp.mean(x * x, axis=-1, keepdims=True)
    return x * lax.rsqrt(ms + EPS) * g


def _rms_mm_kernel(x_ref, g_ref, w_ref, o_ref, xn_ref, *, exact):
    @pl.when(pl.program_id(1) == 0)
    def _():
        xn_ref[...] = _rms(x_ref[...], g_ref[...]).astype(xn_ref.dtype)

    if exact:
        o_ref[...] = _dot3(xn_ref[...], w_ref[...])
    else:
        o_ref[...] = _dot(xn_ref[...], w_ref[...])


def rms_matmul(x, g, w, *, exact, tm, tn):
    n, d = x.shape
    wd = w.shape[1]
    tm = min(tm, n)
    tn = min(tn, wd)
    return pl.pallas_call(
        functools.partial(_rms_mm_kernel, exact=exact),
        grid=(n // tm, wd // tn),
        in_specs=[
            pl.BlockSpec((tm, d), lambda i, j: (i, 0)),
            pl.BlockSpec((1, d), lambda i, j: (0, 0)),
            pl.BlockSpec((d, tn), lambda i, j: (0, j)),
        ],
        out_specs=pl.BlockSpec((tm, tn), lambda i, j: (i, j)),
        out_shape=jax.ShapeDtypeStruct((n, wd), F32),
        scratch_shapes=[pltpu.VMEM((tm, d), F32 if exact else BF16)],
        compiler_params=_cparams(("parallel", "arbitrary")),
        name="rms_matmul",
    )(x, g.reshape(1, d), w)


def _seg_matrix():
    r = lax.broadcasted_iota(jnp.int32, (LANES, LANES), 0) // HD
    c = lax.broadcasted_iota(jnp.int32, (LANES, LANES), 1) // HD
    return jnp.where(r == c, 1.0, 0.0).astype(BF16)


def _subhead_norm(x, g, seg):
    ss = _dot2_exact_rhs(x * x, seg)
    return x * lax.rsqrt(ss * (1.0 / HD) + EPS) * g


def _qk_norm_kernel(q_ref, k_ref, v_ref, g_ref, qb_ref, kn_ref, kb_ref, vb_ref):
    seg = _seg_matrix()
    gq = g_ref[0:1, :]
    gk = g_ref[1:2, :]
    for h in range(N_AH):
        sl = slice(h * LANES, (h + 1) * LANES)
        qn = _subhead_norm(q_ref[:, sl], gq, seg)
        kn = _subhead_norm(k_ref[:, sl], gk, seg)
        qb_ref[:, sl] = (qn * (HD ** -0.5)).astype(qb_ref.dtype)
        kn_ref[:, sl] = kn
        kb_ref[:, sl] = kn.astype(BF16)
    vb_ref[...] = v_ref[...].astype(BF16)


def qk_norm(y, g2, *, tm, q_dtype=BF16):
    n = y.shape[0]
    tm = min(tm, n)
    w = ATTN_W
    col = lambda c: pl.BlockSpec((tm, w), lambda i: (i, c))
    out = pl.BlockSpec((tm, w), lambda i: (i, 0))
    return pl.pallas_call(
        _qk_norm_kernel,
        grid=(n // tm,),
        in_specs=[col(COL_Q), col(COL_K), col(COL_V), pl.BlockSpec((2, LANES), lambda i: (0, 0))],
        out_specs=[out, out, out, out],
        out_shape=[
            jax.ShapeDtypeStruct((n, w), q_dtype),
            jax.ShapeDtypeStruct((n, w), F32),
            jax.ShapeDtypeStruct((n, w), BF16),
            jax.ShapeDtypeStruct((n, w), BF16),
        ],
        compiler_params=_cparams(("parallel",)),
        name="qk_norm",
    )(y, y, y, g2)


def _diff_lambda(lv, lam_init):
    t1 = jnp.sum(lv[0:1, :] * lv[1:2, :], axis=-1, keepdims=True)
    t2 = jnp.sum(lv[2:3, :] * lv[3:4, :], axis=-1, keepdims=True)
    return jnp.exp(t1) - jnp.exp(t2) + lam_init


def _attn_kernel(slope_ref, lv_ref, sg_ref, q_ref, k_ref, v_ref, o_ref, *, tq, lam_init):
    h = pl.program_id(1)
    qi = pl.program_id(2)
    slope = slope_ref[h]
    q = q_ref[...]
    lane = lax.broadcasted_iota(jnp.int32, (tq, LANES), 1)
    first = lane < HD
    col = lax.broadcasted_iota(jnp.int32, (1, tq), 1)
    row2 = lax.broadcasted_iota(jnp.int32, (tq, tq), 0)
    col2 = lax.broadcasted_iota(jnp.int32, (tq, tq), 1)

    def update(s, vv, m, l, acc):
        m_new = jnp.maximum(m, jnp.max(s, axis=-1, keepdims=True))
        alpha = jnp.exp(m - m_new)
        p = jnp.exp(s - m_new)
        l = l * alpha + jnp.sum(p, axis=-1, keepdims=True)
        acc = acc * alpha + _dot(p.astype(BF16), vv)
        return m_new, l, acc

    def block(j, carry, masked):
        m0, l0, a0, m1, l1, a1 = carry
        start = pl.multiple_of(j * tq, tq)
        kk = k_ref[pl.ds(start, tq), :]
        vv = v_ref[pl.ds(start, tq), :]
        zero = jnp.zeros_like(kk)
        k0 = jnp.where(first, kk, zero)
        k1 = jnp.where(first, zero, kk)
        bias = slope * ((j - qi) * tq + col).astype(F32)
        s0 = _dot_nt(q, k0) + bias
        s1 = _dot_nt(q, k1) + bias
        if masked:
            keep = col2 <= row2
            s0 = jnp.where(keep, s0, NEG)
            s1 = jnp.where(keep, s1, NEG)
        m0, l0, a0 = update(s0, vv, m0, l0, a0)
        m1, l1, a1 = update(s1, vv, m1, l1, a1)
        return m0, l0, a0, m1, l1, a1

    mi = jnp.full((tq, 1), NEG, F32)
    li = jnp.zeros((tq, 1), F32)
    ai = jnp.zeros((tq, LANES), F32)
    carry = (mi, li, ai, mi, li, ai)
    carry = lax.fori_loop(0, qi, lambda j, c: block(j, c, False), carry)
    m0, l0, a0, m1, l1, a1 = block(qi, carry, True)

    lam = _diff_lambda(lv_ref[...], lam_init)
    o = a0 / l0 - lam * (a1 / l1)
    o_ref[...] = _rms(o, sg_ref[...]) * (1.0 - lam_init)


def attn_prompt(qb, kb, vb, lv, sg, slopes, *, b, s, lam_init, tq):
    n = b * s
    tq = min(tq, s)
    nq = s // tq
    return pl.pallas_call(
        functools.partial(_attn_kernel, tq=tq, lam_init=lam_init),
        grid=(b, N_AH, nq),
        in_specs=[
            pl.BlockSpec(memory_space=pltpu.SMEM),
            pl.BlockSpec((4, HD), lambda bi, h, i: (0, 0)),
            pl.BlockSpec((1, VD), lambda bi, h, i: (0, 0)),
            pl.BlockSpec((tq, LANES), lambda bi, h, i: (bi * nq + i, h)),
            pl.BlockSpec((s, LANES), lambda bi, h, i: (bi, h)),
            pl.BlockSpec((s, LANES), lambda bi, h, i: (bi, h)),
        ],
        out_specs=pl.BlockSpec((tq, LANES), lambda bi, h, i: (bi * nq + i, h)),
        out_shape=jax.ShapeDtypeStruct((n, ATTN_W), F32),
        compiler_params=_cparams(("parallel", "parallel", "arbitrary")),
        name="attn_prompt",
    )(slopes, lv, sg.reshape(1, VD), qb, kb, vb)


def _scan_rows(a, u):
    tt = a.shape[0]
    row = lax.broadcasted_iota(jnp.int32, a.shape, 0)
    s = 1
    while s < tt:
        keep = row >= s
        a_prev = jnp.where(keep, pltpu.roll(a, s, 0), 1.0)
        u_prev = jnp.where(keep, pltpu.roll(u, s, 0), 0.0)
        u = a * u_prev + u
        a = a * a_prev
        s *= 2
    return a, u


def _lru_gates(xc, wa_ref, ba, wx_ref, bx, exact):
    rs, is_ = [], []
    for nb in range(N_LB):
        xs = xc[:, nb * LB:(nb + 1) * LB]
        if exact:
            rs.append(_dot3(xs, wa_ref[nb]))
            is_.append(_dot3(xs, wx_ref[nb]))
        else:
            xb = xs.astype(BF16)
            rs.append(_dot(xb, wa_ref[nb]))
            is_.append(_dot(xb, wx_ref[nb]))
    r = _sigmoid(jnp.concatenate(rs, axis=-1) + ba)
    i = _sigmoid(jnp.concatenate(is_, axis=-1) + bx)
    return r, i


def _lru_au(xc, r, i, lam):
    log_a = -LRU_C * r * _softplus(-lam)
    a = jnp.exp(log_a)
    u = jnp.sqrt(-_expm1(2.0 * log_a)) * (i * xc)
    return a, u


def _lru_kernel(lx_ref, lg_ref, cw_ref, cb_ref, wa_ref, ba_ref, wx_ref, bx_ref, lam_ref,
                y_ref, conv_ref, hl_ref, xbuf, hc, *, tt):
    t = pl.program_id(1)
    pad = SUBLANES

    @pl.when(t == 0)
    def _():
        xbuf[0:pad, :] = jnp.zeros((pad, LRU_W), F32)
        hc[...] = jnp.zeros_like(hc)

    x = lx_ref[...]
    xbuf[pad:pad + tt, :] = x
    cw = cw_ref[...]
    xc = cb_ref[...] + cw[3:4, :] * x
    for j in range(CONV_W - 1):
        back = CONV_W - 1 - j
        xc = xc + cw[j:j + 1, :] * xbuf[pad - back:pad - back + tt, :]
    xbuf[0:pad, :] = x[tt - pad:tt, :]

    r, i = _lru_gates(xc, wa_ref, ba_ref[...], wx_ref, bx_ref[...], False)
    a, u = _lru_au(xc, r, i, lam_ref[...])
    ca, cu = _scan_rows(a, u)
    hseq = ca * hc[0:1, :] + cu
    hc[0:1, :] = hseq[tt - 1:tt, :]
    y_ref[...] = (hseq * _gelu(lg_ref[...])).astype(y_ref.dtype)

    @pl.when(t == pl.num_programs(1) - 1)
    def _():
        conv_ref[...] = x[tt - (CONV_W - 1):tt, :]
        hl_ref[...] = hseq[tt - 1:tt, :]


def lru_prompt(y, cw, cb, wa, ba, wx, bx, lam, *, b, s, tt):
    n = b * s
    tt = min(tt, s)
    nt = s // tt
    vec = lambda: pl.BlockSpec((1, LRU_W), lambda bi, t: (0, 0))
    mat = lambda: pl.BlockSpec((N_LB, LB, LB), lambda bi, t: (0, 0, 0))
    return pl.pallas_call(
        functools.partial(_lru_kernel, tt=tt),
        grid=(b, nt),
        in_specs=[
            pl.BlockSpec((tt, LRU_W), lambda bi, t: (bi * nt + t, COL_LX)),
            pl.BlockSpec((tt, LRU_W), lambda bi, t: (bi * nt + t, COL_LG)),
            pl.BlockSpec((CONV_W, LRU_W), lambda bi, t: (0, 0)),
            vec(), mat(), vec(), mat(), vec(), vec(),
        ],
        out_specs=[
            pl.BlockSpec((tt, LRU_W), lambda bi, t: (bi * nt + t, 0)),
            pl.BlockSpec((None, CONV_W - 1, LRU_W), lambda bi, t: (bi, 0, 0)),
            pl.BlockSpec((None, 1, LRU_W), lambda bi, t: (bi, 0, 0)),
        ],
        out_shape=[
            jax.ShapeDtypeStruct((n, LRU_W), BF16),
            jax.ShapeDtypeStruct((b, CONV_W - 1, LRU_W), F32),
            jax.ShapeDtypeStruct((b, 1, LRU_W), F32),
        ],
        scratch_shapes=[pltpu.VMEM((tt + SUBLANES, LRU_W), F32), pltpu.VMEM((SUBLANES, LRU_W), F32)],
        compiler_params=_cparams(("parallel", "arbitrary")),
        name="lru_prompt",
    )(y, y, cw, cb.reshape(1, -1), wa.astype(BF16), ba.reshape(1, -1), wx.astype(BF16),
      bx.reshape(1, -1), lam.reshape(1, -1))


H_CHUNK = 64
H_SUB = SUBLANES


def _hgrn_lower_bound(rows, layer):
    mx = rows[0]
    for r in rows[1:]:
        mx = jnp.maximum(mx, r)
    es = [jnp.exp(r - mx) for r in rows]
    tot = es[0]
    for e in es[1:]:
        tot = tot + e
    lb = jnp.zeros_like(mx)
    for d in range(1, layer + 1):
        lb = lb + es[d] / tot
    return lb


def _hgrn_log_f(z, lb):
    log_sig = jnp.minimum(z, 0.0) - jnp.log1p(jnp.exp(-jnp.abs(z)))
    a = jnp.log(lb)
    bb = jnp.log1p(-lb) + log_sig
    return jnp.maximum(a, bb) + jnp.log1p(jnp.exp(-jnp.abs(a - bb)))


def _cumsum_rows(x):
    n = x.shape[0]
    row = lax.broadcasted_iota(jnp.int32, x.shape, 0)
    s = 1
    while s < n:
        x = x + jnp.where(row >= s, pltpu.roll(x, s, 0), 0.0)
        s *= 2
    return x


def _hgrn_chunk(qq, kk, logf, vv, st):
    c = qq.shape[0]
    nsub = c // H_SUB
    bcum = _cumsum_rows(logf)
    row_c = lax.broadcasted_iota(jnp.int32, (c, LANES), 0)
    row_s = lax.broadcasted_iota(jnp.int32, (H_SUB, LANES), 0)
    vb = vv.astype(BF16)

    o = _dot_nt((qq * jnp.exp(bcum)).astype(BF16), st.astype(BF16))

    att_rows = []
    diag_rows = []
    for i in range(nsub):
        r0 = i * H_SUB
        q_r = qq[r0:r0 + H_SUB, :]
        b_r = bcum[r0:r0 + H_SUB, :]
        if i == 0:
            att_rows.append(jnp.zeros((H_SUB, c), F32))
        else:
            beta = bcum[r0 - 1:r0, :]
            qt = q_r * jnp.exp(b_r - beta)
            kt = kk * jnp.exp(jnp.where(row_c < r0, beta - bcum, NEG))
            att_rows.append(_dot_nt(qt.astype(BF16), kt.astype(BF16)))
        k_r = kk[r0:r0 + H_SUB, :]
        v_r = vv[r0:r0 + H_SUB, :]
        od = jnp.zeros((H_SUB, LANES), F32)
        for s in range(H_SUB):
            e = jnp.exp(jnp.where(row_s >= s, b_r - b_r[s:s + 1, :], NEG))
            w = jnp.sum(q_r * k_r[s:s + 1, :] * e, axis=-1, keepdims=True)
            od = od + w * v_r[s:s + 1, :]
        diag_rows.append(od)
    att = jnp.concatenate(att_rows, axis=0)
    o = o + _dot(att.astype(BF16), vb) + jnp.concatenate(diag_rows, axis=0)

    b_last = bcum[c - 1:c, :]
    kd = (kk * jnp.exp(b_last - bcum)).astype(BF16)
    st = st * jnp.exp(b_last) + lax.dot_general(vb, kd, (((0,), (0,)), ((), ())),
                                                 preferred_element_type=F32)
    return o, st


def _hgrn_kernel(hq_ref, hf_ref, hi_ref, hg_ref, lb_ref, g_ref, y_ref, s_ref, st_ref, *, layer, nchunk):
    lb = _hgrn_lower_bound([lb_ref[d:d + 1, :] for d in range(DEPTH)], layer)
    g = g_ref[...]
    st_ref[...] = jnp.zeros_like(st_ref)

    def body(ci, carry):
        r0 = pl.multiple_of(ci * H_CHUNK, H_CHUNK)
        rows = pl.ds(r0, H_CHUNK)
        z = hf_ref[rows, :]
        logf = _hgrn_log_f(z, lb)
        kk = (1.0 - lb) * _sigmoid(-z)
        qq = _silu(hq_ref[rows, :])
        o, st = _hgrn_chunk(qq, kk, logf, hi_ref[rows, :], st_ref[...])
        st_ref[...] = st
        y_ref[rows, :] = (_rms(o, g) * _silu(hg_ref[rows, :])).astype(y_ref.dtype)
        return carry

    lax.fori_loop(0, nchunk, body, 0)
    s_ref[...] = st_ref[...]


def hgrn_prompt(y, lb_all, g, *, b, s, layer):
    n = b * s
    nchunk = s // H_CHUNK
    col = lambda c: pl.BlockSpec((s, LANES), lambda bi, h: (bi, c * N_HH + h))
    return pl.pallas_call(
        functools.partial(_hgrn_kernel, layer=layer, nchunk=nchunk),
        grid=(b, N_HH),
        in_specs=[
            col(COL_HQ), col(COL_HF), col(COL_HI), col(COL_HG),
            pl.BlockSpec((DEPTH, LANES), lambda bi, h: (0, h)),
            pl.BlockSpec((1, HV), lambda bi, h: (0, 0)),
        ],
        out_specs=[
            pl.BlockSpec((s, LANES), lambda bi, h: (bi, h)),
            pl.BlockSpec((None, None, HV, HK), lambda bi, h: (bi, h, 0, 0)),
        ],
        out_shape=[
            jax.ShapeDtypeStruct((n, HGRN_W), BF16),
            jax.ShapeDtypeStruct((b, N_HH, HV, HK), F32),
        ],
        scratch_shapes=[pltpu.VMEM((HV, HK), F32)],
        compiler_params=_cparams(("parallel", "parallel")),
        name="hgrn_prompt",
    )(y, y, y, y, lb_all, g.reshape(1, HV))


def _merge_kernel(x_ref, ya_ref, yl_ref, yh_ref, ga_ref, gl_ref, gh_ref,
                  wa_ref, wl_ref, wh_ref, wo_ref, o_ref, *, exact):
    if exact:
        mm = _dot3
        cast = lambda a: a.astype(F32)
    else:
        mm = _dot
        cast = lambda a: a.astype(BF16)
    m = _sigmoid(ga_ref[...]) * mm(cast(ya_ref[...]), wa_ref[...])
    m = m + _sigmoid(gl_ref[...]) * mm(cast(yl_ref[...]), wl_ref[...])
    m = m + _sigmoid(gh_ref[...]) * mm(cast(yh_ref[...]), wh_ref[...])
    o_ref[...] = x_ref[...] + mm(cast(m), wo_ref[...])


def merge_out(x, y, ya, yl, yh, wa, wl, wh, wo, *, exact, tm):
    n = x.shape[0]
    tm = min(tm, n)
    d = D_MODEL
    row = lambda: pl.BlockSpec((tm, d), lambda i: (i, 0))
    gate = lambda c: pl.BlockSpec((tm, d), lambda i: (i, COL_GT + c))
    wspec = lambda: pl.BlockSpec((d, d), lambda i: (0, 0))
    wdt = F32 if exact else BF16
    return pl.pallas_call(
        functools.partial(_merge_kernel, exact=exact),
        grid=(n // tm,),
        in_specs=[row(), row(), row(), row(), gate(0), gate(1), gate(2), wspec(), wspec(), wspec(), wspec()],
        out_specs=row(),
        out_shape=jax.ShapeDtypeStruct((n, d), F32),
        compiler_params=_cparams(("parallel",)),
        name="merge_out",
    )(x, ya, yl, yh, y, y, y, wa.astype(wdt), wl.astype(wdt), wh.astype(wdt), wo.astype(wdt))


def _topk_rows(x, k):
    out = []
    cur = x
    for _ in range(k):
        mx = jnp.max(cur, axis=0, keepdims=True)
        out.append(mx)
        cur = jnp.where(cur == mx, NEG, cur)
    return out


def _peer_select_kernel(q_ref, kh_ref, sel_ref):
    st = _dot3_nt(kh_ref[...], q_ref[...])
    s1 = st[:N_KEYS, :]
    s2 = st[N_KEYS:, :]
    v1 = _topk_rows(s1, P_TOPK)
    v2 = _topk_rows(s2, P_TOPK)
    v2m = jnp.concatenate(v2, axis=0)
    cands = [v1[a] + v2m for a in range(P_TOPK)]

    cur = cands
    tau = None
    for _ in range(P_TOPK):
        mx = cur[0]
        for c in cur[1:]:
            mx = jnp.maximum(mx, c)
        tau = jnp.max(mx, axis=0, keepdims=True)
        cur = [jnp.where(c == tau, NEG, c) for c in cur]

    top = v1[0] + v2[0]
    z = jnp.zeros_like(tau)
    theta_dense = jnp.full(s1.shape, -NEG, F32)
    for a in range(P_TOPK):
        sel = cands[a] >= tau
        z = z + jnp.sum(jnp.where(sel, jnp.exp(cands[a] - top), 0.0), axis=0, keepdims=True)
        theta_a = jnp.min(jnp.where(sel, v2m, -NEG), axis=0, keepdims=True)
        theta_dense = jnp.where(s1 == v1[a], theta_a, theta_dense)
    sel_ref[0] = s2
    sel_ref[1] = jnp.exp(s2 - v2[0])
    sel_ref[2] = theta_dense
    sel_ref[3] = jnp.exp(s1 - v1[0]) / z


def peer_select(q, kbd, *, tn):
    n = q.shape[0]
    tn = min(tn, n)
    return pl.pallas_call(
        _peer_select_kernel,
        grid=(n // tn, P_HEADS),
        in_specs=[
            pl.BlockSpec((tn, D_KEY), lambda i, h: (i, h)),
            pl.BlockSpec((None, 2 * N_KEYS, D_KEY), lambda i, h: (h, 0, 0)),
        ],
        out_specs=pl.BlockSpec((None, 4, N_KEYS, tn), lambda i, h: (h, 0, 0, i)),
        out_shape=jax.ShapeDtypeStruct((P_HEADS, 4, N_KEYS, n), F32),
        compiler_params=_cparams(("parallel", "parallel")),
        name="peer_select",
    )(q, kbd)


def _peer_dense_kernel(xn_ref, x_ref, sel_ref, u_ref, vt_ref, o_ref, acc_ref, *, ec):
    e = pl.program_id(1)

    @pl.when(e == 0)
    def _():
        acc_ref[...] = jnp.zeros_like(acc_ref)

    a_t = _dot_nt(u_ref[...], xn_ref[...])
    n_i1 = ec // N_KEYS
    rows = []
    for ii in range(n_i1):
        i1 = e * n_i1 + ii
        w = None
        for h in range(P_HEADS):
            theta = sel_ref[h, 2, pl.ds(i1, 1), :]
            c1 = sel_ref[h, 3, pl.ds(i1, 1), :]
            term = jnp.where(sel_ref[h, 0] >= theta, sel_ref[h, 1], 0.0) * c1
            w = term if w is None else w + term
        rows.append(w)
    w_t = jnp.concatenate(rows, axis=0) if n_i1 > 1 else rows[0]
    h_t = (w_t * _gelu(a_t)).astype(BF16)
    acc_ref[...] += _dot(vt_ref[...], h_t)

    @pl.when(e == pl.num_programs(1) - 1)
    def _():
        o_ref[...] = x_ref[...] + acc_ref[...].T


def peer_dense(xn, x, sel, u, vt, *, tn, ec):
    n = x.shape[0]
    tn = min(tn, n)
    d = D_MODEL
    return pl.pallas_call(
        functools.partial(_peer_dense_kernel, ec=ec),
        grid=(n // tn, N_EXP // ec),
        in_specs=[
            pl.BlockSpec((tn, d), lambda i, e: (i, 0)),
            pl.BlockSpec((tn, d), lambda i, e: (i, 0)),
            pl.BlockSpec((P_HEADS, 4, N_KEYS, tn), lambda i, e: (0, 0, 0, i)),
            pl.BlockSpec((ec, d), lambda i, e: (e, 0)),
            pl.BlockSpec((d, ec), lambda i, e: (0, e)),
        ],
        out_specs=pl.BlockSpec((tn, d), lambda i, e: (i, 0)),
        out_shape=jax.ShapeDtypeStruct((n, d), F32),
        scratch_shapes=[pltpu.VMEM((d, tn), F32)],
        compiler_params=_cparams(("parallel", "arbitrary")),
        name="peer_dense",
    )(xn, x, sel, u, vt)


def _rms_cast_kernel(x_ref, g_ref, o_ref):
    o_ref[...] = _rms(x_ref[...], g_ref[...]).astype(o_ref.dtype)


def rms_cast(x, g, *, tm):
    n, d = x.shape
    tm = min(tm, n)
    return pl.pallas_call(
        _rms_cast_kernel,
        grid=(n // tm,),
        in_specs=[pl.BlockSpec((tm, d), lambda i: (i, 0)), pl.BlockSpec((1, d), lambda i: (0, 0))],
        out_specs=pl.BlockSpec((tm, d), lambda i: (i, 0)),
        out_shape=jax.ShapeDtypeStruct((n, d), BF16),
        compiler_params=_cparams(("parallel",)),
        name="rms_cast",
    )(x, g.reshape(1, d))


def _peer_key_blocks(keys):
    z = jnp.zeros_like(keys[:, 0])
    top = jnp.concatenate([keys[:, 0], z], axis=-1)
    bot = jnp.concatenate([z, keys[:, 1]], axis=-1)
    return jnp.concatenate([top, bot], axis=1)


def peer_ffn(x, g, wq, kbd, u_b, vt_b, *, tn_sel, tn, ec):
    n = x.shape[0]
    pad = (-n) % LANES
    xp = jnp.pad(x, ((0, pad), (0, 0))) if pad else x
    q = rms_matmul(xp, g, wq, exact=True, tm=512, tn=1024)
    sel = peer_select(q, kbd, tn=tn_sel)
    xn = rms_cast(xp, g, tm=512)
    out = peer_dense(xn, xp, sel, u_b, vt_b, tn=tn, ec=ec)
    return out[:n] if pad else out


def _lam_init(layer):
    return 0.8 - 0.6 * math.exp(-0.3 * layer)


def _alibi_slopes():
    return jnp.exp2(-8.0 * jnp.arange(1, N_AH + 1, dtype=F32) / N_AH)


def layer_prompt(x, l, p, w_in_b, kbd, u_b, vt_b, *, b, s):
    n = b * s
    y = rms_matmul(x, p['norm1_g'][l], w_in_b, exact=False, tm=1024, tn=1024)
    g2 = jnp.tile(p['qk_norm_g'][l], (1, 2))
    qb, kn, kb, vb = qk_norm(y, g2, tm=512)
    ya = attn_prompt(qb, kb, vb, p['diff_lambda'][l], p['subln_g'][l], _alibi_slopes(),
                     b=b, s=s, lam_init=_lam_init(l), tq=256)
    yl, conv_new, h_last = lru_prompt(y, p['conv_w'][l], p['conv_b'][l], p['rg_wa'][l], p['rg_ba'][l],
                                      p['rg_wx'][l], p['rg_bx'][l], p['rg_lambda'][l], b=b, s=s, tt=256)
    yh, st = hgrn_prompt(y, p['hgrn_lb'], p['hgrn_norm_g'][l], b=b, s=s, layer=l)
    x = merge_out(x, y, ya, yl, yh, p['w_br_attn'][l], p['w_br_lru'][l], p['w_br_hgrn'][l],
                  p['w_out'][l], exact=False, tm=512)
    x = peer_ffn(x, p['norm2_g'][l], p['peer_wq'][l], kbd, u_b, vt_b, tn_sel=256, tn=512, ec=512)
    k_out = kn.reshape(b, s, N_AH, 2 * HD)
    v_out = y[:, COL_V * D_MODEL:(COL_V + 1) * D_MODEL].reshape(b, s, N_AH, VD)
    s_out = jnp.swapaxes(st, -1, -2)
    return x, (k_out, v_out, conv_new, h_last.reshape(b, LRU_W), s_out)


DEC_PPS = 4


def _seg2_matrix():
    r = lax.broadcasted_iota(jnp.int32, (LANES, 2 * LANES), 0) // HD
    c = lax.broadcasted_iota(jnp.int32, (LANES, 2 * LANES), 1) // LANES
    return jnp.where(r == c, 1.0, 0.0).astype(BF16)


def _attn_decode_kernel(pt_ref, q_ref, kn_ref, vn_ref, slope_ref, lv_ref, sg_ref, *refs,
                        pps, past, lam_init):
    k_refs = refs[:pps]
    v_refs = refs[pps:2 * pps]
    o_ref, m_ref, l_ref, acc_ref = refs[2 * pps:]
    step = pl.program_id(1)

    @pl.when(step == 0)
    def _():
        m_ref[...] = jnp.full(m_ref.shape, NEG, F32)
        l_ref[...] = jnp.zeros_like(l_ref)
        acc_ref[...] = jnp.zeros_like(acc_ref)

    q = q_ref[...]
    seg2 = _seg2_matrix()
    slope = slope_ref[...]
    tpos = lax.broadcasted_iota(jnp.int32, (PAGE_SIZE, N_AH, LANES), 0)
    for i in range(pps):
        kpage = k_refs[i][...]
        vpage = v_refs[i][...]
        prod = (kpage * q[None]).reshape(PAGE_SIZE * N_AH, LANES).astype(BF16)
        s = _dot(prod, seg2)
        kpos = tpos + ((step * pps + i) * PAGE_SIZE - past)
        bias = slope[None] * kpos.astype(F32)
        for c in range(2):
            sc = s[:, c * LANES:(c + 1) * LANES].reshape(PAGE_SIZE, N_AH, LANES) + bias
            m_old = m_ref[c]
            m_new = jnp.maximum(m_old, jnp.max(sc, axis=0))
            alpha = jnp.exp(m_old - m_new)
            pw = jnp.exp(sc - m_new[None])
            l_ref[c] = l_ref[c] * alpha + jnp.sum(pw, axis=0)
            acc_ref[c] = acc_ref[c] * alpha + jnp.sum(pw * vpage, axis=0)
            m_ref[c] = m_new

    @pl.when(step == pl.num_programs(1) - 1)
    def _():
        vn = vn_ref[...]
        sn = _dot2_exact_rhs(kn_ref[...] * q, seg2)
        outs = []
        for c in range(2):
            sc = sn[:, c * LANES:(c + 1) * LANES]
            m_old = m_ref[c]
            m_new = jnp.maximum(m_old, sc)
            alpha = jnp.exp(m_old - m_new)
            pw = jnp.exp(sc - m_new)
            l = l_ref[c] * alpha + pw
            acc = acc_ref[c] * alpha + pw * vn
            outs.append(acc / l)
        lam = _diff_lambda(lv_ref[...], lam_init)
        o = outs[0] - lam * outs[1]
        o_ref[...] = _rms(o, sg_ref[...]) * (1.0 - lam_init)


def attn_decode(page_table, q, kn, vn, cache_k, cache_v, lv, sg, slopes, *, layer, lam_init):
    bd, n_pages = page_table.shape
    pps = DEC_PPS
    past = n_pages * PAGE_SIZE
    tok = lambda: pl.BlockSpec((None, N_AH, LANES), lambda b, st, pt: (b, 0, 0))
    page = lambda i: pl.BlockSpec((None, None, PAGE_SIZE, N_AH, LANES),
                                  lambda b, st, pt: (layer, pt[b, st * pps + i], 0, 0, 0))
    grid_spec = pltpu.PrefetchScalarGridSpec(
        num_scalar_prefetch=1,
        grid=(bd, n_pages // pps),
        in_specs=[
            tok(), tok(), tok(),
            pl.BlockSpec((N_AH, LANES), lambda b, st, pt: (0, 0)),
            pl.BlockSpec((4, HD), lambda b, st, pt: (0, 0)),
            pl.BlockSpec((1, VD), lambda b, st, pt: (0, 0)),
        ] + [page(i) for i in range(pps)] + [page(i) for i in range(pps)],
        out_specs=pl.BlockSpec((None, N_AH, LANES), lambda b, st, pt: (b, 0, 0)),
        scratch_shapes=[pltpu.VMEM((2, N_AH, LANES), F32)] * 3,
    )
    slope_tile = jnp.broadcast_to(slopes[:, None], (N_AH, LANES))
    return pl.pallas_call(
        functools.partial(_attn_decode_kernel, pps=pps, past=past, lam_init=lam_init),
        grid_spec=grid_spec,
        out_shape=jax.ShapeDtypeStruct((bd, N_AH, LANES), F32),
        compiler_params=_cparams(("parallel", "arbitrary")),
        name="attn_decode",
    )(page_table, q, kn, vn, slope_tile, lv, sg.reshape(1, VD), *([cache_k] * pps), *([cache_v] * pps))


def _lru_step_kernel(lx_ref, lg_ref, buf_ref, h0_ref, cw_ref, cb_ref, wa_ref, ba_ref, wx_ref, bx_ref,
                     lam_ref, yl_ref, conv_ref, h_ref):
    x = lx_ref[...]
    cw = cw_ref[...]
    xc = cb_ref[...] + cw[CONV_W - 1:CONV_W, :] * x
    for j in range(CONV_W - 1):
        xc = xc + cw[j:j + 1, :] * buf_ref[j]
    r, i = _lru_gates(xc, wa_ref, ba_ref[...], wx_ref, bx_ref[...], True)
    a, u = _lru_au(xc, r, i, lam_ref[...])
    h = a * h0_ref[...] + u
    yl_ref[...] = h * _gelu(lg_ref[...])
    for j in range(CONV_W - 2):
        conv_ref[j] = buf_ref[j + 1]
    conv_ref[CONV_W - 2] = x
    h_ref[...] = h


def lru_step(y, buf_t, h0, cw, cb, wa, ba, wx, bx, lam):
    bd = y.shape[0]
    full = lambda shape: pl.BlockSpec(shape, lambda i: (0,) * len(shape))
    return pl.pallas_call(
        _lru_step_kernel,
        grid=(1,),
        in_specs=[
            pl.BlockSpec((bd, LRU_W), lambda i: (0, COL_LX)),
            pl.BlockSpec((bd, LRU_W), lambda i: (0, COL_LG)),
            full((CONV_W - 1, bd, LRU_W)), full((bd, LRU_W)), full((CONV_W, LRU_W)), full((1, LRU_W)),
            full((N_LB, LB, LB)), full((1, LRU_W)), full((N_LB, LB, LB)), full((1, LRU_W)), full((1, LRU_W)),
        ],
        out_specs=[full((bd, LRU_W)), full((CONV_W - 1, bd, LRU_W)), full((bd, LRU_W))],
        out_shape=[
            jax.ShapeDtypeStruct((bd, LRU_W), F32),
            jax.ShapeDtypeStruct((CONV_W - 1, bd, LRU_W), F32),
            jax.ShapeDtypeStruct((bd, LRU_W), F32),
        ],
        compiler_params=_cparams(("arbitrary",)),
        name="lru_step",
    )(y, y, buf_t, h0, cw, cb.reshape(1, -1), wa, ba.reshape(1, -1), wx, bx.reshape(1, -1), lam.reshape(1, -1))


def _hgrn_step_kernel(hq_ref, hf_ref, hi_ref, hg_ref, lb_ref, g_ref, s0_ref, y_ref, s1_ref, *, layer):
    lb = _hgrn_lower_bound([lb_ref[d] for d in range(DEPTH)], layer)
    z = hf_ref[...]
    f = jnp.exp(_hgrn_log_f(z, lb))
    kk = (1.0 - lb) * _sigmoid(-z)
    qq = _silu(hq_ref[...])
    vv = hi_ref[...]
    pad = jnp.zeros((HK - N_HH, LANES), F32)
    cols = lambda a: jnp.concatenate([a, pad], axis=0).T
    q_t, f_t, k_t = cols(qq), cols(f), cols(kk)
    row = lax.broadcasted_iota(jnp.int32, (N_HH, LANES), 0)
    o = jnp.zeros((N_HH, LANES), F32)
    for h in range(N_HH):
        s1 = s0_ref[h] * f_t[:, h:h + 1] + k_t[:, h:h + 1] * vv[h:h + 1, :]
        s1_ref[h] = s1
        oh = jnp.sum(q_t[:, h:h + 1] * s1, axis=0, keepdims=True)
        o = jnp.where(row == h, oh, o)
    y_ref[...] = _rms(o, g_ref[...]) * _silu(hg_ref[...])


def hgrn_step(y3, lb3, g, state, *, layer):
    bd = y3.shape[0]
    col = lambda c: pl.BlockSpec((None, N_HH, LANES), lambda b: (b, c, 0))
    return pl.pallas_call(
        functools.partial(_hgrn_step_kernel, layer=layer),
        grid=(bd,),
        in_specs=[
            col(COL_HQ), col(COL_HF), col(COL_HI), col(COL_HG),
            pl.BlockSpec((DEPTH, N_HH, LANES), lambda b: (0, 0, 0)),
            pl.BlockSpec((1, HV), lambda b: (0, 0)),
            pl.BlockSpec((None, None, N_HH, HK, HV), lambda b: (layer, b, 0, 0, 0)),
        ],
        out_specs=[
            pl.BlockSpec((None, N_HH, LANES), lambda b: (b, 0, 0)),
            pl.BlockSpec((None, N_HH, HK, HV), lambda b: (b, 0, 0, 0)),
        ],
        out_shape=[
            jax.ShapeDtypeStruct((bd, N_HH, LANES), F32),
            jax.ShapeDtypeStruct((bd, N_HH, HK, HV), F32),
        ],
        compiler_params=_cparams(("parallel",)),
        name="hgrn_step",
    )(y3, y3, y3, y3, lb3, g.reshape(1, HV), state)


def layer_sample(x, l, p, cache_k, cache_v, page_table, state_conv, state_lru, state_hgrn, kbd, u_b, vt_b):
    bd = x.shape[0]
    y = rms_matmul(x, p['norm1_g'][l], p['w_in'][l], exact=True, tm=bd, tn=1024)
    g2 = jnp.tile(p['qk_norm_g'][l], (1, 2))
    qf, kn, _, _ = qk_norm(y, g2, tm=bd, q_dtype=F32)
    v_new = y[:, COL_V * D_MODEL:(COL_V + 1) * D_MODEL]
    tok = lambda a: a.reshape(bd, N_AH, LANES)
    ya = attn_decode(page_table, tok(qf), tok(kn), tok(v_new), cache_k, cache_v, p['diff_lambda'][l],
                     p['subln_g'][l], _alibi_slopes(), layer=l, lam_init=_lam_init(l))
    yl, conv_t, h_new = lru_step(y, jnp.swapaxes(state_conv[l], 0, 1), state_lru[l], p['conv_w'][l],
                                 p['conv_b'][l], p['rg_wa'][l], p['rg_ba'][l], p['rg_wx'][l], p['rg_bx'][l],
                                 p['rg_lambda'][l])
    yh, s1 = hgrn_step(y.reshape(bd, IN_W // LANES, LANES), p['hgrn_lb'].reshape(DEPTH, N_HH, LANES),
                       p['hgrn_norm_g'][l], state_hgrn, layer=l)
    x = merge_out(x, y, ya.reshape(bd, ATTN_W), yl, yh.reshape(bd, HGRN_W), p['w_br_attn'][l],
                  p['w_br_lru'][l], p['w_br_hgrn'][l], p['w_out'][l], exact=True, tm=bd)
    x = peer_ffn(x, p['norm2_g'][l], p['peer_wq'][l], kbd, u_b, vt_b, tn_sel=256, tn=512, ec=512)
    return x, (kn.reshape(bd, 1, N_AH, 2 * HD), v_new.reshape(bd, 1, N_AH, VD), jnp.swapaxes(conv_t, 0, 1),
               h_new, s1)


def kernel(x_prompt, x_sample, cache_k, cache_v, state_conv, state_lru, state_hgrn, page_table,
           norm1_g, norm2_g, w_in, qk_norm_g, diff_lambda, subln_g, conv_w, conv_b,
           rg_wa, rg_ba, rg_wx, rg_bx, rg_lambda, hgrn_lb, hgrn_norm_g,
           w_br_attn, w_br_lru, w_br_hgrn, w_out, peer_wq, peer_keys, peer_u, peer_v):
    p = dict(norm1_g=norm1_g, norm2_g=norm2_g, w_in=w_in, qk_norm_g=qk_norm_g, diff_lambda=diff_lambda,
             subln_g=subln_g, conv_w=conv_w, conv_b=conv_b, rg_wa=rg_wa, rg_ba=rg_ba, rg_wx=rg_wx,
             rg_bx=rg_bx, rg_lambda=rg_lambda, hgrn_lb=hgrn_lb, hgrn_norm_g=hgrn_norm_g,
             w_br_attn=w_br_attn, w_br_lru=w_br_lru, w_br_hgrn=w_br_hgrn, w_out=w_out,
             peer_wq=peer_wq, peer_keys=peer_keys, peer_u=peer_u, peer_v=peer_v)
    b, s, _ = x_prompt.shape
    bd = x_sample.shape[0]
    xp = x_prompt.reshape(b * s, D_MODEL)
    xs = x_sample.reshape(bd, D_MODEL)
    st_p, st_s = [], []
    for l in range(DEPTH):
        w_in_b = w_in[l].astype(BF16)
        kbd = _peer_key_blocks(peer_keys[l])
        u_b = peer_u[l].astype(BF16)
        vt_b = peer_v[l].T.astype(BF16)
        xp, sp = layer_prompt(xp, l, p, w_in_b, kbd, u_b, vt_b, b=b, s=s)
        xs, ss = layer_sample(xs, l, p, cache_k, cache_v, page_table, state_conv, state_lru, state_hgrn,
                              kbd, u_b, vt_b)
        st_p.append(sp)
        st_s.append(ss)
    k_p, v_p, conv_p, lru_p, hgrn_p = [jnp.stack(a) for a in zip(*st_p)]
    k_s, v_s, conv_s, lru_s, hgrn_s = [jnp.stack(a) for a in zip(*st_s)]
    return (xp.reshape(b, s, D_MODEL), xs.reshape(bd, 1, D_MODEL), k_p, v_p, conv_p, lru_p, hgrn_p,
            k_s, v_s, conv_s, lru_s, hgrn_s)
```

```python
import functools
import math

import jax
import jax.numpy as jnp
from jax import lax
from jax.experimental import pallas as pl
from jax.experimental.pallas import tpu as pltpu

F32 = jnp.float32
BF16 = jnp.bfloat16

D_MODEL = 1024
DEPTH = 2
PAGE_SIZE = 128
N_AH = 8
HD = 64
VD = 2 * HD
ATTN_W = N_AH * VD
LRU_W = 1024
N_LB = 8
LB = LRU_W // N_LB
CONV_W = 4
LRU_C = 8.0
N_HH = 8
HK = 128
HV = 128
HGRN_W = N_HH * HV
P_HEADS = 8
N_KEYS = 128
N_EXP = N_KEYS * N_KEYS
D_KEY = 128
P_TOPK = 16
EPS = 1e-6
IN_W = 12 * D_MODEL

COL_Q, COL_K, COL_V, COL_LX, COL_LG, COL_HQ, COL_HF, COL_HI, COL_HG, COL_GT = 0, 1, 2, 3, 4, 5, 6, 7, 8, 9

LANES = 128
SUBLANES = 8
VMEM_LIMIT = 56 * 1024 * 1024

NEG = -1e30
SQRT_HALF = 0.7071067811865476


def _cparams(sem, flags=None):
    return pltpu.CompilerParams(dimension_semantics=sem, vmem_limit_bytes=VMEM_LIMIT, flags=flags)


def _split(a):
    hi = a.astype(BF16)
    lo = (a - hi.astype(F32)).astype(BF16)
    return hi, lo


def _dot(a, b):
    return jnp.dot(a, b, preferred_element_type=F32)


def _dot_nt(a, b):
    return lax.dot_general(a, b, (((1,), (1,)), ((), ())), preferred_element_type=F32)


def _dot3(a, b):
    ah, al = _split(a)
    bh, bl = _split(b)
    return _dot(ah, bh) + _dot(al, bh) + _dot(ah, bl)


def _dot3_nt(a, b):
    ah, al = _split(a)
    bh, bl = _split(b)
    return _dot_nt(ah, bh) + _dot_nt(al, bh) + _dot_nt(ah, bl)


def _dot2_exact_rhs(a, b_bf16):
    ah, al = _split(a)
    return _dot(ah, b_bf16) + _dot(al, b_bf16)


def _sigmoid(x):
    return 1.0 / (1.0 + jnp.exp(-x))


def _gelu(x):
    return 0.5 * x * (1.0 + lax.erf(x * SQRT_HALF))


def _silu(x):
    return x * _sigmoid(x)


def _softplus(x):
    return jnp.maximum(x, 0.0) + jnp.log1p(jnp.exp(-jnp.abs(x)))


def _expm1(x):
    u = jnp.exp(x)
    um1 = u - 1.0
    corrected = um1 * x / jnp.log(u)
    return jnp.where(um1 == 0.0, x, jnp.where(um1 == -1.0, -1.0, corrected))


def _rms(x, g):
    ms = jnp.mean(x * x, axis=-1, keepdims=True)
    return x * lax.rsqrt(ms + EPS) * g


def _rms_mm_kernel(x_ref, g_ref, w_ref, o_ref, xn_ref, *, exact):
    @pl.when(pl.program_id(1) == 0)
    def _():
        xn_ref[...] = _rms(x_ref[...], g_ref[...]).astype(xn_ref.dtype)

    if exact:
        o_ref[...] = _dot3(xn_ref[...], w_ref[...])
    else:
        o_ref[...] = _dot(xn_ref[...], w_ref[...])


def rms_matmul(x, g, w, *, exact, tm, tn):
    n, d = x.shape
    wd = w.shape[1]
    tm = min(tm, n)
    tn = min(tn, wd)
    return pl.pallas_call(
        functools.partial(_rms_mm_kernel, exact=exact),
        grid=(n // tm, wd // tn),
        in_specs=[
            pl.BlockSpec((tm, d), lambda i, j: (i, 0)),
            pl.BlockSpec((1, d), lambda i, j: (0, 0)),
            pl.BlockSpec((d, tn), lambda i, j: (0, j)),
        ],
        out_specs=pl.BlockSpec((tm, tn), lambda i, j: (i, j)),
        out_shape=jax.ShapeDtypeStruct((n, wd), F32),
        scratch_shapes=[pltpu.VMEM((tm, d), F32 if exact else BF16)],
        compiler_params=_cparams(("parallel", "arbitrary")),
        name="rms_matmul",
    )(x, g.reshape(1, d), w)


def _seg_matrix():
    r = lax.broadcasted_iota(jnp.int32, (LANES, LANES), 0) // HD
    c = lax.broadcasted_iota(jnp.int32, (LANES, LANES), 1) // HD
    return jnp.where(r == c, 1.0, 0.0).astype(BF16)


def _subhead_norm(x, g, seg):
    ss = _dot2_exact_rhs(x * x, seg)
    return x * lax.rsqrt(ss * (1.0 / HD) + EPS) * g


def _qk_norm_kernel(q_ref, k_ref, v_ref, g_ref, qb_ref, kn_ref, kb_ref, vb_ref):
    seg = _seg_matrix()
    gq = g_ref[0:1, :]
    gk = g_ref[1:2, :]
    for h in range(N_AH):
        sl = slice(h * LANES, (h + 1) * LANES)
        qn = _subhead_norm(q_ref[:, sl], gq, seg)
        kn = _subhead_norm(k_ref[:, sl], gk, seg)
        qb_ref[:, sl] = (qn * (HD ** -0.5)).astype(qb_ref.dtype)
        kn_ref[:, sl] = kn
        kb_ref[:, sl] = kn.astype(BF16)
    vb_ref[...] = v_ref[...].astype(BF16)


def qk_norm(y, g2, *, tm, q_dtype=BF16):
    n = y.shape[0]
    tm = min(tm, n)
    w = ATTN_W
    col = lambda c: pl.BlockSpec((tm, w), lambda i: (i, c))
    out = pl.BlockSpec((tm, w), lambda i: (i, 0))
    return pl.pallas_call(
        _qk_norm_kernel,
        grid=(n // tm,),
        in_specs=[col(COL_Q), col(COL_K), col(COL_V), pl.BlockSpec((2, LANES), lambda i: (0, 0))],
        out_specs=[out, out, out, out],
        out_shape=[
            jax.ShapeDtypeStruct((n, w), q_dtype),
            jax.ShapeDtypeStruct((n, w), F32),
            jax.ShapeDtypeStruct((n, w), BF16),
            jax.ShapeDtypeStruct((n, w), BF16),
        ],
        compiler_params=_cparams(("parallel",)),
        name="qk_norm",
    )(y, y, y, g2)


def _diff_lambda(lv, lam_init):
    t1 = jnp.sum(lv[0:1, :] * lv[1:2, :], axis=-1, keepdims=True)
    t2 = jnp.sum(lv[2:3, :] * lv[3:4, :], axis=-1, keepdims=True)
    return jnp.exp(t1) - jnp.exp(t2) + lam_init


def _attn_kernel(slope_ref, lv_ref, sg_ref, q_ref, k_ref, v_ref, o_ref, m_ref, acc_ref, *, tq, lam_init):
    h = pl.program_id(1)
    qi = pl.program_id(2)
    slope = slope_ref[h]
    q = q_ref[...]
    first = lax.broadcasted_iota(jnp.int32, (tq, LANES), 1) < HD
    col = lax.broadcasted_iota(jnp.int32, (1, tq), 1)
    ones = jnp.ones((tq, LANES), BF16)
    m_ref[...] = jnp.full(m_ref.shape, NEG, F32)
    acc_ref[...] = jnp.zeros_like(acc_ref)

    def update(c, s, v1):
        m_old = m_ref[c]
        m_new = jnp.maximum(m_old, jnp.max(s, axis=-1, keepdims=True))
        p = jnp.exp(s - m_new).astype(BF16)
        acc_ref[c] = acc_ref[c] * jnp.exp(m_old - m_new) + _dot(p, v1)
        m_ref[c] = m_new

    def block(j, masked):
        start = pl.multiple_of(j * tq, tq)
        kk = k_ref[pl.ds(start, tq), :]
        v1 = jnp.concatenate([v_ref[pl.ds(start, tq), :], ones], axis=-1)
        zero = jnp.zeros_like(kk)
        bias = slope * ((j - qi) * tq + col).astype(F32)
        for c in range(2):
            kc = jnp.where(first, kk, zero) if c == 0 else jnp.where(first, zero, kk)
            s = _dot_nt(q, kc) + bias
            if masked:
                row2 = lax.broadcasted_iota(jnp.int32, (tq, tq), 0)
                col2 = lax.broadcasted_iota(jnp.int32, (tq, tq), 1)
                s = jnp.where(col2 <= row2, s, NEG)
            update(c, s, v1)

    def body(j, carry):
        block(j, False)
        return carry

    lax.fori_loop(0, qi, body, 0)
    block(qi, True)

    lam = _diff_lambda(lv_ref[...], lam_init)
    r0 = acc_ref[0]
    r1 = acc_ref[1]
    o = r0[:, :LANES] / r0[:, LANES:] - lam * (r1[:, :LANES] / r1[:, LANES:])
    o_ref[...] = _rms(o, sg_ref[...]) * (1.0 - lam_init)


def attn_prompt(qb, kb, vb, lv, sg, slopes, *, b, s, lam_init, tq):
    n = b * s
    tq = min(tq, s)
    nq = s // tq
    return pl.pallas_call(
        functools.partial(_attn_kernel, tq=tq, lam_init=lam_init),
        grid=(b, N_AH, nq),
        in_specs=[
            pl.BlockSpec(memory_space=pltpu.SMEM),
            pl.BlockSpec((4, HD), lambda bi, h, i: (0, 0)),
            pl.BlockSpec((1, VD), lambda bi, h, i: (0, 0)),
            pl.BlockSpec((tq, LANES), lambda bi, h, i: (bi * nq + i, h)),
            pl.BlockSpec((s, LANES), lambda bi, h, i: (bi, h)),
            pl.BlockSpec((s, LANES), lambda bi, h, i: (bi, h)),
        ],
        out_specs=pl.BlockSpec((tq, LANES), lambda bi, h, i: (bi * nq + i, h)),
        out_shape=jax.ShapeDtypeStruct((n, ATTN_W), F32),
        scratch_shapes=[pltpu.VMEM((2, tq, 1), F32), pltpu.VMEM((2, tq, 2 * LANES), F32)],
        compiler_params=_cparams(("parallel", "parallel", "arbitrary")),
        name="attn_prompt",
    )(slopes, lv, sg.reshape(1, VD), qb, kb, vb)


def _scan_rows(a, u):
    tt = a.shape[0]
    row = lax.broadcasted_iota(jnp.int32, a.shape, 0)
    s = 1
    while s < tt:
        keep = row >= s
        a_prev = jnp.where(keep, pltpu.roll(a, s, 0), 1.0)
        u_prev = jnp.where(keep, pltpu.roll(u, s, 0), 0.0)
        u = a * u_prev + u
        a = a * a_prev
        s *= 2
    return a, u


def _lru_gates(xc, wa_ref, ba, wx_ref, bx, exact):
    rs, is_ = [], []
    for nb in range(N_LB):
        xs = xc[:, nb * LB:(nb + 1) * LB]
        if exact:
            rs.append(_dot3(xs, wa_ref[nb]))
            is_.append(_dot3(xs, wx_ref[nb]))
        else:
            xb = xs.astype(BF16)
            rs.append(_dot(xb, wa_ref[nb]))
            is_.append(_dot(xb, wx_ref[nb]))
    r = _sigmoid(jnp.concatenate(rs, axis=-1) + ba)
    i = _sigmoid(jnp.concatenate(is_, axis=-1) + bx)
    return r, i


def _lru_au(xc, r, i, lam):
    log_a = -LRU_C * r * _softplus(-lam)
    a = jnp.exp(log_a)
    u = jnp.sqrt(-_expm1(2.0 * log_a)) * (i * xc)
    return a, u


def _lru_kernel(lx_ref, lg_ref, cw_ref, cb_ref, wa_ref, ba_ref, wx_ref, bx_ref, lam_ref,
                y_ref, conv_ref, hl_ref, xbuf, hc, *, tt):
    t = pl.program_id(1)
    pad = SUBLANES

    @pl.when(t == 0)
    def _():
        xbuf[0:pad, :] = jnp.zeros((pad, LRU_W), F32)
        hc[...] = jnp.zeros_like(hc)

    x = lx_ref[...]
    xbuf[pad:pad + tt, :] = x
    cw = cw_ref[...]
    xc = cb_ref[...] + cw[3:4, :] * x
    for j in range(CONV_W - 1):
        back = CONV_W - 1 - j
        xc = xc + cw[j:j + 1, :] * xbuf[pad - back:pad - back + tt, :]
    xbuf[0:pad, :] = x[tt - pad:tt, :]

    r, i = _lru_gates(xc, wa_ref, ba_ref[...], wx_ref, bx_ref[...], False)
    a, u = _lru_au(xc, r, i, lam_ref[...])
    ca, cu = _scan_rows(a, u)
    hseq = ca * hc[0:1, :] + cu
    hc[0:1, :] = hseq[tt - 1:tt, :]
    y_ref[...] = (hseq * _gelu(lg_ref[...])).astype(y_ref.dtype)

    @pl.when(t == pl.num_programs(1) - 1)
    def _():
        conv_ref[...] = x[tt - (CONV_W - 1):tt, :]
        hl_ref[...] = hseq[tt - 1:tt, :]


def lru_prompt(y, cw, cb, wa, ba, wx, bx, lam, *, b, s, tt):
    n = b * s
    tt = min(tt, s)
    nt = s // tt
    vec = lambda: pl.BlockSpec((1, LRU_W), lambda bi, t: (0, 0))
    mat = lambda: pl.BlockSpec((N_LB, LB, LB), lambda bi, t: (0, 0, 0))
    return pl.pallas_call(
        functools.partial(_lru_kernel, tt=tt),
        grid=(b, nt),
        in_specs=[
            pl.BlockSpec((tt, LRU_W), lambda bi, t: (bi * nt + t, COL_LX)),
            pl.BlockSpec((tt, LRU_W), lambda bi, t: (bi * nt + t, COL_LG)),
            pl.BlockSpec((CONV_W, LRU_W), lambda bi, t: (0, 0)),
            vec(), mat(), vec(), mat(), vec(), vec(),
        ],
        out_specs=[
            pl.BlockSpec((tt, LRU_W), lambda bi, t: (bi * nt + t, 0)),
            pl.BlockSpec((None, CONV_W - 1, LRU_W), lambda bi, t: (bi, 0, 0)),
            pl.BlockSpec((None, 1, LRU_W), lambda bi, t: (bi, 0, 0)),
        ],
        out_shape=[
            jax.ShapeDtypeStruct((n, LRU_W), BF16),
            jax.ShapeDtypeStruct((b, CONV_W - 1, LRU_W), F32),
            jax.ShapeDtypeStruct((b, 1, LRU_W), F32),
        ],
        scratch_shapes=[pltpu.VMEM((tt + SUBLANES, LRU_W), F32), pltpu.VMEM((SUBLANES, LRU_W), F32)],
        compiler_params=_cparams(("parallel", "arbitrary")),
        name="lru_prompt",
    )(y, y, cw, cb.reshape(1, -1), wa.astype(BF16), ba.reshape(1, -1), wx.astype(BF16),
      bx.reshape(1, -1), lam.reshape(1, -1))


H_CHUNK = 64
H_SUB = SUBLANES


def _hgrn_lower_bound(rows, layer):
    mx = rows[0]
    for r in rows[1:]:
        mx = jnp.maximum(mx, r)
    es = [jnp.exp(r - mx) for r in rows]
    tot = es[0]
    for e in es[1:]:
        tot = tot + e
    lb = jnp.zeros_like(mx)
    for d in range(1, layer + 1):
        lb = lb + es[d] / tot
    return lb


def _hgrn_log_f(z, lb):
    log_sig = jnp.minimum(z, 0.0) - jnp.log1p(jnp.exp(-jnp.abs(z)))
    a = jnp.log(lb)
    bb = jnp.log1p(-lb) + log_sig
    return jnp.maximum(a, bb) + jnp.log1p(jnp.exp(-jnp.abs(a - bb)))


def _cumsum_rows(x):
    n = x.shape[0]
    row = lax.broadcasted_iota(jnp.int32, x.shape, 0)
    s = 1
    while s < n:
        x = x + jnp.where(row >= s, pltpu.roll(x, s, 0), 0.0)
        s *= 2
    return x


def _hgrn_chunk(qq, kk, logf, vv, st):
    c = qq.shape[0]
    nsub = c // H_SUB
    bcum = _cumsum_rows(logf)
    row_c = lax.broadcasted_iota(jnp.int32, (c, LANES), 0)
    row_s = lax.broadcasted_iota(jnp.int32, (H_SUB, LANES), 0)
    vb = vv.astype(BF16)

    o = _dot_nt((qq * jnp.exp(bcum)).astype(BF16), st.astype(BF16))

    att_rows = []
    diag_rows = []
    for i in range(nsub):
        r0 = i * H_SUB
        q_r = qq[r0:r0 + H_SUB, :]
        b_r = bcum[r0:r0 + H_SUB, :]
        if i == 0:
            att_rows.append(jnp.zeros((H_SUB, c), F32))
        else:
            beta = bcum[r0 - 1:r0, :]
            qt = q_r * jnp.exp(b_r - beta)
            kt = kk * jnp.exp(jnp.where(row_c < r0, beta - bcum, NEG))
            att_rows.append(_dot_nt(qt.astype(BF16), kt.astype(BF16)))
        k_r = kk[r0:r0 + H_SUB, :]
        v_r = vv[r0:r0 + H_SUB, :]
        od = jnp.zeros((H_SUB, LANES), F32)
        for s in range(H_SUB):
            e = jnp.exp(jnp.where(row_s >= s, b_r - b_r[s:s + 1, :], NEG))
            w = jnp.sum(q_r * k_r[s:s + 1, :] * e, axis=-1, keepdims=True)
            od = od + w * v_r[s:s + 1, :]
        diag_rows.append(od)
    att = jnp.concatenate(att_rows, axis=0)
    o = o + _dot(att.astype(BF16), vb) + jnp.concatenate(diag_rows, axis=0)

    b_last = bcum[c - 1:c, :]
    kd = (kk * jnp.exp(b_last - bcum)).astype(BF16)
    st = st * jnp.exp(b_last) + lax.dot_general(vb, kd, (((0,), (0,)), ((), ())),
                                                 preferred_element_type=F32)
    return o, st


def _hgrn_kernel(hq_ref, hf_ref, hi_ref, hg_ref, lb_ref, g_ref, y_ref, s_ref, st_ref, *, layer, nchunk):
    lb = _hgrn_lower_bound([lb_ref[d:d + 1, :] for d in range(DEPTH)], layer)
    g = g_ref[...]
    st_ref[...] = jnp.zeros_like(st_ref)

    def body(ci, carry):
        r0 = pl.multiple_of(ci * H_CHUNK, H_CHUNK)
        rows = pl.ds(r0, H_CHUNK)
        z = hf_ref[rows, :]
        logf = _hgrn_log_f(z, lb)
        kk = (1.0 - lb) * _sigmoid(-z)
        qq = _silu(hq_ref[rows, :])
        o, st = _hgrn_chunk(qq, kk, logf, hi_ref[rows, :], st_ref[...])
        st_ref[...] = st
        y_ref[rows, :] = (_rms(o, g) * _silu(hg_ref[rows, :])).astype(y_ref.dtype)
        return carry

    lax.fori_loop(0, nchunk, body, 0, unroll=2)
    s_ref[...] = st_ref[...]


def hgrn_prompt(y, lb_all, g, *, b, s, layer):
    n = b * s
    nchunk = s // H_CHUNK
    col = lambda c: pl.BlockSpec((s, LANES), lambda bi, h: (bi, c * N_HH + h))
    return pl.pallas_call(
        functools.partial(_hgrn_kernel, layer=layer, nchunk=nchunk),
        grid=(b, N_HH),
        in_specs=[
            col(COL_HQ), col(COL_HF), col(COL_HI), col(COL_HG),
            pl.BlockSpec((DEPTH, LANES), lambda bi, h: (0, h)),
            pl.BlockSpec((1, HV), lambda bi, h: (0, 0)),
        ],
        out_specs=[
            pl.BlockSpec((s, LANES), lambda bi, h: (bi, h)),
            pl.BlockSpec((None, None, HV, HK), lambda bi, h: (bi, h, 0, 0)),
        ],
        out_shape=[
            jax.ShapeDtypeStruct((n, HGRN_W), BF16),
            jax.ShapeDtypeStruct((b, N_HH, HV, HK), F32),
        ],
        scratch_shapes=[pltpu.VMEM((HV, HK), F32)],
        compiler_params=_cparams(("parallel", "parallel")),
        name="hgrn_prompt",
    )(y, y, y, y, lb_all, g.reshape(1, HV))


def _merge_kernel(x_ref, ya_ref, yl_ref, yh_ref, ga_ref, gl_ref, gh_ref,
                  wa_ref, wl_ref, wh_ref, wo_ref, o_ref, *, exact):
    if exact:
        mm = _dot3
        cast = lambda a: a.astype(F32)
    else:
        mm = _dot
        cast = lambda a: a.astype(BF16)
    m = _sigmoid(ga_ref[...]) * mm(cast(ya_ref[...]), wa_ref[...])
    m = m + _sigmoid(gl_ref[...]) * mm(cast(yl_ref[...]), wl_ref[...])
    m = m + _sigmoid(gh_ref[...]) * mm(cast(yh_ref[...]), wh_ref[...])
    o_ref[...] = x_ref[...] + mm(cast(m), wo_ref[...])


def merge_out(x, y, ya, yl, yh, wa, wl, wh, wo, *, exact, tm):
    n = x.shape[0]
    tm = min(tm, n)
    d = D_MODEL
    row = lambda: pl.BlockSpec((tm, d), lambda i: (i, 0))
    gate = lambda c: pl.BlockSpec((tm, d), lambda i: (i, COL_GT + c))
    wspec = lambda: pl.BlockSpec((d, d), lambda i: (0, 0))
    wdt = F32 if exact else BF16
    return pl.pallas_call(
        functools.partial(_merge_kernel, exact=exact),
        grid=(n // tm,),
        in_specs=[row(), row(), row(), row(), gate(0), gate(1), gate(2), wspec(), wspec(), wspec(), wspec()],
        out_specs=row(),
        out_shape=jax.ShapeDtypeStruct((n, d), F32),
        compiler_params=_cparams(("parallel",)),
        name="merge_out",
    )(x, ya, yl, yh, y, y, y, wa.astype(wdt), wl.astype(wdt), wh.astype(wdt), wo.astype(wdt))


def _topk_rows(x, k):
    out = []
    cur = x
    for _ in range(k):
        mx = jnp.max(cur, axis=0, keepdims=True)
        out.append(mx)
        cur = jnp.where(cur == mx, NEG, cur)
    return out


def _peer_select_kernel(q_ref, kh_ref, sel_ref):
    st = _dot3_nt(kh_ref[...], q_ref[...])
    s1 = st[:N_KEYS, :]
    s2 = st[N_KEYS:, :]
    v1 = _topk_rows(s1, P_TOPK)
    v2 = _topk_rows(s2, P_TOPK)
    v2m = jnp.concatenate(v2, axis=0)
    cands = [v1[a] + v2m for a in range(P_TOPK)]

    cur = cands
    tau = None
    for _ in range(P_TOPK):
        mx = cur[0]
        for c in cur[1:]:
            mx = jnp.maximum(mx, c)
        tau = jnp.max(mx, axis=0, keepdims=True)
        cur = [jnp.where(c == tau, NEG, c) for c in cur]

    top = v1[0] + v2[0]
    z = jnp.zeros_like(tau)
    theta_dense = jnp.full(s1.shape, -NEG, F32)
    for a in range(P_TOPK):
        sel = cands[a] >= tau
        z = z + jnp.sum(jnp.where(sel, jnp.exp(cands[a] - top), 0.0), axis=0, keepdims=True)
        theta_a = jnp.min(jnp.where(sel, v2m, -NEG), axis=0, keepdims=True)
        theta_dense = jnp.where(s1 == v1[a], theta_a, theta_dense)
    sel_ref[0] = s2
    sel_ref[1] = jnp.exp(s2 - v2[0])
    sel_ref[2] = theta_dense
    sel_ref[3] = jnp.exp(s1 - v1[0]) / z


def peer_select(q, kbd, *, tn):
    n = q.shape[0]
    tn = min(tn, n)
    return pl.pallas_call(
        _peer_select_kernel,
        grid=(n // tn, P_HEADS),
        in_specs=[
            pl.BlockSpec((tn, D_KEY), lambda i, h: (i, h)),
            pl.BlockSpec((None, 2 * N_KEYS, D_KEY), lambda i, h: (h, 0, 0)),
        ],
        out_specs=pl.BlockSpec((None, 4, N_KEYS, tn), lambda i, h: (h, 0, 0, i)),
        out_shape=jax.ShapeDtypeStruct((P_HEADS, 4, N_KEYS, n), F32),
        compiler_params=_cparams(("parallel", "parallel")),
        name="peer_select",
    )(q, kbd)


def _peer_dense_kernel(xn_ref, x_ref, sel_ref, u_ref, vt_ref, o_ref, acc_ref, w_ref, sp_ref, bc_ref, *, ec):
    e = pl.program_id(1)
    tn = w_ref.shape[1]
    n_i1 = ec // N_KEYS
    tiles = [slice(lt * LANES, (lt + 1) * LANES) for lt in range(tn // LANES)]

    @pl.when(e == 0)
    def _():
        acc_ref[...] = jnp.zeros_like(acc_ref)
        for h in range(P_HEADS):
            for k in range(2):
                sp_ref[h, k, 0:N_KEYS, 0:tn] = sel_ref[h, k]

    for ii in range(n_i1):
        i1 = e * n_i1 + ii
        for h in range(P_HEADS):
            for k in range(2):
                bc_ref[ii, h, k, :, 0:tn] = jnp.broadcast_to(sel_ref[h, 2 + k, pl.ds(i1, 1), :], (SUBLANES, tn))

    def rows_body(r, carry):
        r0 = pl.multiple_of(r * SUBLANES, SUBLANES)
        w = [[None] * n_i1 for _ in tiles]
        for h in range(P_HEADS):
            s2 = [sp_ref[h, 0, pl.ds(r0, SUBLANES), lanes] for lanes in tiles]
            p2 = [sp_ref[h, 1, pl.ds(r0, SUBLANES), lanes] for lanes in tiles]
            for ii in range(n_i1):
                for lt, lanes in enumerate(tiles):
                    term = jnp.where(s2[lt] >= bc_ref[ii, h, 0, :, lanes], p2[lt], 0.0) * bc_ref[ii, h, 1, :, lanes]
                    w[lt][ii] = term if w[lt][ii] is None else w[lt][ii] + term
        for lt, lanes in enumerate(tiles):
            for ii in range(n_i1):
                w_ref[pl.ds(ii * N_KEYS + r0, SUBLANES), lanes] = w[lt][ii]
        return carry

    lax.fori_loop(0, N_KEYS // SUBLANES, rows_body, 0)
    a_t = _dot_nt(u_ref[...], xn_ref[...])
    h_t = (w_ref[...] * _gelu(a_t)).astype(BF16)
    acc_ref[...] += _dot(vt_ref[...], h_t)

    @pl.when(e == pl.num_programs(1) - 1)
    def _():
        o_ref[...] = x_ref[...] + acc_ref[...].T


def peer_dense(xn, x, sel, u, vt, *, tn, ec):
    n = x.shape[0]
    tn = min(tn, n)
    d = D_MODEL
    return pl.pallas_call(
        functools.partial(_peer_dense_kernel, ec=ec),
        grid=(n // tn, N_EXP // ec),
        in_specs=[
            pl.BlockSpec((tn, d), lambda i, e: (i, 0)),
            pl.BlockSpec((tn, d), lambda i, e: (i, 0)),
            pl.BlockSpec((P_HEADS, 4, N_KEYS, tn), lambda i, e: (0, 0, 0, i)),
            pl.BlockSpec((ec, d), lambda i, e: (e, 0)),
            pl.BlockSpec((d, ec), lambda i, e: (0, e)),
        ],
        out_specs=pl.BlockSpec((tn, d), lambda i, e: (i, 0)),
        out_shape=jax.ShapeDtypeStruct((n, d), F32),
        scratch_shapes=[pltpu.VMEM((d, tn), F32), pltpu.VMEM((ec, tn), F32),
                        pltpu.VMEM((P_HEADS, 2, N_KEYS + SUBLANES, tn + LANES), F32),
                        pltpu.VMEM((ec // N_KEYS, P_HEADS, 2, SUBLANES, tn + LANES), F32)],
        compiler_params=_cparams(("parallel", "arbitrary")),
        name="peer_dense",
    )(xn, x, sel, u, vt)


def _rms_cast_kernel(x_ref, g_ref, o_ref):
    o_ref[...] = _rms(x_ref[...], g_ref[...]).astype(o_ref.dtype)


def rms_cast(x, g, *, tm):
    n, d = x.shape
    tm = min(tm, n)
    return pl.pallas_call(
        _rms_cast_kernel,
        grid=(n // tm,),
        in_specs=[pl.BlockSpec((tm, d), lambda i: (i, 0)), pl.BlockSpec((1, d), lambda i: (0, 0))],
        out_specs=pl.BlockSpec((tm, d), lambda i: (i, 0)),
        out_shape=jax.ShapeDtypeStruct((n, d), BF16),
        compiler_params=_cparams(("parallel",)),
        name="rms_cast",
    )(x, g.reshape(1, d))


def _peer_key_blocks(keys):
    z = jnp.zeros_like(keys[:, 0])
    top = jnp.concatenate([keys[:, 0], z], axis=-1)
    bot = jnp.concatenate([z, keys[:, 1]], axis=-1)
    return jnp.concatenate([top, bot], axis=1)


def peer_ffn(x, g, wq, kbd, u_b, vt_b, *, tn_sel, tn, ec):
    n = x.shape[0]
    pad = (-n) % LANES
    xp = jnp.pad(x, ((0, pad), (0, 0))) if pad else x
    q = rms_matmul(xp, g, wq, exact=True, tm=512, tn=1024)
    sel = peer_select(q, kbd, tn=tn_sel)
    xn = rms_cast(xp, g, tm=512)
    out = peer_dense(xn, xp, sel, u_b, vt_b, tn=tn, ec=ec)
    return out[:n] if pad else out


def _lam_init(layer):
    return 0.8 - 0.6 * math.exp(-0.3 * layer)


def _alibi_slopes():
    return jnp.exp2(-8.0 * jnp.arange(1, N_AH + 1, dtype=F32) / N_AH)


def layer_prompt(x, l, p, w_in_b, kbd, u_b, vt_b, *, b, s):
    n = b * s
    y = rms_matmul(x, p['norm1_g'][l], w_in_b, exact=False, tm=1024, tn=1024)
    g2 = jnp.tile(p['qk_norm_g'][l], (1, 2))
    qb, kn, kb, vb = qk_norm(y, g2, tm=512)
    ya = attn_prompt(qb, kb, vb, p['diff_lambda'][l], p['subln_g'][l], _alibi_slopes(),
                     b=b, s=s, lam_init=_lam_init(l), tq=512)
    yl, conv_new, h_last = lru_prompt(y, p['conv_w'][l], p['conv_b'][l], p['rg_wa'][l], p['rg_ba'][l],
                                      p['rg_wx'][l], p['rg_bx'][l], p['rg_lambda'][l], b=b, s=s, tt=256)
    yh, st = hgrn_prompt(y, p['hgrn_lb'], p['hgrn_norm_g'][l], b=b, s=s, layer=l)
    x = merge_out(x, y, ya, yl, yh, p['w_br_attn'][l], p['w_br_lru'][l], p['w_br_hgrn'][l],
                  p['w_out'][l], exact=False, tm=512)
    x = peer_ffn(x, p['norm2_g'][l], p['peer_wq'][l], kbd, u_b, vt_b, tn_sel=256, tn=512, ec=512)
    k_out = kn.reshape(b, s, N_AH, 2 * HD)
    v_out = y[:, COL_V * D_MODEL:(COL_V + 1) * D_MODEL].reshape(b, s, N_AH, VD)
    s_out = jnp.swapaxes(st, -1, -2)
    return x, (k_out, v_out, conv_new, h_last.reshape(b, LRU_W), s_out)


DEC_PPS = 4


def _seg2_matrix():
    r = lax.broadcasted_iota(jnp.int32, (LANES, 2 * LANES), 0) // HD
    c = lax.broadcasted_iota(jnp.int32, (LANES, 2 * LANES), 1) // LANES
    return jnp.where(r == c, 1.0, 0.0).astype(BF16)


def _attn_decode_kernel(pt_ref, q_ref, kn_ref, vn_ref, slope_ref, lv_ref, sg_ref, *refs,
                        pps, past, lam_init):
    k_refs = refs[:pps]
    v_refs = refs[pps:2 * pps]
    o_ref, m_ref, l_ref, acc_ref = refs[2 * pps:]
    step = pl.program_id(1)

    @pl.when(step == 0)
    def _():
        m_ref[...] = jnp.full(m_ref.shape, NEG, F32)
        l_ref[...] = jnp.zeros_like(l_ref)
        acc_ref[...] = jnp.zeros_like(acc_ref)

    q = q_ref[...]
    seg2 = _seg2_matrix()
    slope = slope_ref[...]
    tpos = lax.broadcasted_iota(jnp.int32, (PAGE_SIZE, N_AH, LANES), 0)
    for i in range(pps):
        kpage = k_refs[i][...]
        vpage = v_refs[i][...]
        prod = (kpage * q[None]).reshape(PAGE_SIZE * N_AH, LANES).astype(BF16)
        s = _dot(prod, seg2)
        kpos = tpos + ((step * pps + i) * PAGE_SIZE - past)
        bias = slope[None] * kpos.astype(F32)
        for c in range(2):
            sc = s[:, c * LANES:(c + 1) * LANES].reshape(PAGE_SIZE, N_AH, LANES) + bias
            m_old = m_ref[c]
            m_new = jnp.maximum(m_old, jnp.max(sc, axis=0))
            alpha = jnp.exp(m_old - m_new)
            pw = jnp.exp(sc - m_new[None])
            l_ref[c] = l_ref[c] * alpha + jnp.sum(pw, axis=0)
            acc_ref[c] = acc_ref[c] * alpha + jnp.sum(pw * vpage, axis=0)
            m_ref[c] = m_new

    @pl.when(step == pl.num_programs(1) - 1)
    def _():
        vn = vn_ref[...]
        sn = _dot2_exact_rhs(kn_ref[...] * q, seg2)
        outs = []
        for c in range(2):
            sc = sn[:, c * LANES:(c + 1) * LANES]
            m_old = m_ref[c]
            m_new = jnp.maximum(m_old, sc)
            alpha = jnp.exp(m_old - m_new)
            pw = jnp.exp(sc - m_new)
            l = l_ref[c] * alpha + pw
            acc = acc_ref[c] * alpha + pw * vn
            outs.append(acc / l)
        lam = _diff_lambda(lv_ref[...], lam_init)
        o = outs[0] - lam * outs[1]
        o_ref[...] = _rms(o, sg_ref[...]) * (1.0 - lam_init)


def attn_decode(page_table, q, kn, vn, cache_k, cache_v, lv, sg, slopes, *, layer, lam_init):
    bd, n_pages = page_table.shape
    pps = DEC_PPS
    past = n_pages * PAGE_SIZE
    tok = lambda: pl.BlockSpec((None, N_AH, LANES), lambda b, st, pt: (b, 0, 0))
    page = lambda i: pl.BlockSpec((None, None, PAGE_SIZE, N_AH, LANES),
                                  lambda b, st, pt: (layer, pt[b, st * pps + i], 0, 0, 0))
    grid_spec = pltpu.PrefetchScalarGridSpec(
        num_scalar_prefetch=1,
        grid=(bd, n_pages // pps),
        in_specs=[
            tok(), tok(), tok(),
            pl.BlockSpec((N_AH, LANES), lambda b, st, pt: (0, 0)),
            pl.BlockSpec((4, HD), lambda b, st, pt: (0, 0)),
            pl.BlockSpec((1, VD), lambda b, st, pt: (0, 0)),
        ] + [page(i) for i in range(pps)] + [page(i) for i in range(pps)],
        out_specs=pl.BlockSpec((None, N_AH, LANES), lambda b, st, pt: (b, 0, 0)),
        scratch_shapes=[pltpu.VMEM((2, N_AH, LANES), F32)] * 3,
    )
    slope_tile = jnp.broadcast_to(slopes[:, None], (N_AH, LANES))
    return pl.pallas_call(
        functools.partial(_attn_decode_kernel, pps=pps, past=past, lam_init=lam_init),
        grid_spec=grid_spec,
        out_shape=jax.ShapeDtypeStruct((bd, N_AH, LANES), F32),
        compiler_params=_cparams(("parallel", "arbitrary")),
        name="attn_decode",
    )(page_table, q, kn, vn, slope_tile, lv, sg.reshape(1, VD), *([cache_k] * pps), *([cache_v] * pps))


def _lru_step_kernel(lx_ref, lg_ref, buf_ref, h0_ref, cw_ref, cb_ref, wa_ref, ba_ref, wx_ref, bx_ref,
                     lam_ref, yl_ref, conv_ref, h_ref):
    x = lx_ref[...]
    cw = cw_ref[...]
    xc = cb_ref[...] + cw[CONV_W - 1:CONV_W, :] * x
    for j in range(CONV_W - 1):
        xc = xc + cw[j:j + 1, :] * buf_ref[j]
    r, i = _lru_gates(xc, wa_ref, ba_ref[...], wx_ref, bx_ref[...], True)
    a, u = _lru_au(xc, r, i, lam_ref[...])
    h = a * h0_ref[...] + u
    yl_ref[...] = h * _gelu(lg_ref[...])
    for j in range(CONV_W - 2):
        conv_ref[j] = buf_ref[j + 1]
    conv_ref[CONV_W - 2] = x
    h_ref[...] = h


def lru_step(y, buf_t, h0, cw, cb, wa, ba, wx, bx, lam):
    bd = y.shape[0]
    full = lambda shape: pl.BlockSpec(shape, lambda i: (0,) * len(shape))
    return pl.pallas_call(
        _lru_step_kernel,
        grid=(1,),
        in_specs=[
            pl.BlockSpec((bd, LRU_W), lambda i: (0, COL_LX)),
            pl.BlockSpec((bd, LRU_W), lambda i: (0, COL_LG)),
            full((CONV_W - 1, bd, LRU_W)), full((bd, LRU_W)), full((CONV_W, LRU_W)), full((1, LRU_W)),
            full((N_LB, LB, LB)), full((1, LRU_W)), full((N_LB, LB, LB)), full((1, LRU_W)), full((1, LRU_W)),
        ],
        out_specs=[full((bd, LRU_W)), full((CONV_W - 1, bd, LRU_W)), full((bd, LRU_W))],
        out_shape=[
            jax.ShapeDtypeStruct((bd, LRU_W), F32),
            jax.ShapeDtypeStruct((CONV_W - 1, bd, LRU_W), F32),
            jax.ShapeDtypeStruct((bd, LRU_W), F32),
        ],
        compiler_params=_cparams(("arbitrary",)),
        name="lru_step",
    )(y, y, buf_t, h0, cw, cb.reshape(1, -1), wa, ba.reshape(1, -1), wx, bx.reshape(1, -1), lam.reshape(1, -1))


def _hgrn_step_kernel(hq_ref, hf_ref, hi_ref, hg_ref, lb_ref, g_ref, s0_ref, y_ref, s1_ref, *, layer):
    lb = _hgrn_lower_bound([lb_ref[d] for d in range(DEPTH)], layer)
    z = hf_ref[...]
    f = jnp.exp(_hgrn_log_f(z, lb))
    kk = (1.0 - lb) * _sigmoid(-z)
    qq = _silu(hq_ref[...])
    vv = hi_ref[...]
    pad = jnp.zeros((HK - N_HH, LANES), F32)
    cols = lambda a: jnp.concatenate([a, pad], axis=0).T
    q_t, f_t, k_t = cols(qq), cols(f), cols(kk)
    row = lax.broadcasted_iota(jnp.int32, (N_HH, LANES), 0)
    o = jnp.zeros((N_HH, LANES), F32)
    for h in range(N_HH):
        s1 = s0_ref[h] * f_t[:, h:h + 1] + k_t[:, h:h + 1] * vv[h:h + 1, :]
        s1_ref[h] = s1
        oh = jnp.sum(q_t[:, h:h + 1] * s1, axis=0, keepdims=True)
        o = jnp.where(row == h, oh, o)
    y_ref[...] = _rms(o, g_ref[...]) * _silu(hg_ref[...])


def hgrn_step(y3, lb3, g, state, *, layer):
    bd = y3.shape[0]
    col = lambda c: pl.BlockSpec((None, N_HH, LANES), lambda b: (b, c, 0))
    return pl.pallas_call(
        functools.partial(_hgrn_step_kernel, layer=layer),
        grid=(bd,),
        in_specs=[
            col(COL_HQ), col(COL_HF), col(COL_HI), col(COL_HG),
            pl.BlockSpec((DEPTH, N_HH, LANES), lambda b: (0, 0, 0)),
            pl.BlockSpec((1, HV), lambda b: (0, 0)),
            pl.BlockSpec((None, None, N_HH, HK, HV), lambda b: (layer, b, 0, 0, 0)),
        ],
        out_specs=[
            pl.BlockSpec((None, N_HH, LANES), lambda b: (b, 0, 0)),
            pl.BlockSpec((None, N_HH, HK, HV), lambda b: (b, 0, 0, 0)),
        ],
        out_shape=[
            jax.ShapeDtypeStruct((bd, N_HH, LANES), F32),
            jax.ShapeDtypeStruct((bd, N_HH, HK, HV), F32),
        ],
        compiler_params=_cparams(("parallel",)),
        name="hgrn_step",
    )(y3, y3, y3, y3, lb3, g.reshape(1, HV), state)


def layer_sample(x, l, p, cache_k, cache_v, page_table, state_conv, state_lru, state_hgrn, kbd, u_b, vt_b):
    bd = x.shape[0]
    y = rms_matmul(x, p['norm1_g'][l], p['w_in'][l], exact=True, tm=bd, tn=1024)
    g2 = jnp.tile(p['qk_norm_g'][l], (1, 2))
    qf, kn, _, _ = qk_norm(y, g2, tm=bd, q_dtype=F32)
    v_new = y[:, COL_V * D_MODEL:(COL_V + 1) * D_MODEL]
    tok = lambda a: a.reshape(bd, N_AH, LANES)
    ya = attn_decode(page_table, tok(qf), tok(kn), tok(v_new), cache_k, cache_v, p['diff_lambda'][l],
                     p['subln_g'][l], _alibi_slopes(), layer=l, lam_init=_lam_init(l))
    yl, conv_t, h_new = lru_step(y, jnp.swapaxes(state_conv[l], 0, 1), state_lru[l], p['conv_w'][l],
                                 p['conv_b'][l], p['rg_wa'][l], p['rg_ba'][l], p['rg_wx'][l], p['rg_bx'][l],
                                 p['rg_lambda'][l])
    yh, s1 = hgrn_step(y.reshape(bd, IN_W // LANES, LANES), p['hgrn_lb'].reshape(DEPTH, N_HH, LANES),
                       p['hgrn_norm_g'][l], state_hgrn, layer=l)
    x = merge_out(x, y, ya.reshape(bd, ATTN_W), yl, yh.reshape(bd, HGRN_W), p['w_br_attn'][l],
                  p['w_br_lru'][l], p['w_br_hgrn'][l], p['w_out'][l], exact=True, tm=bd)
    x = peer_ffn(x, p['norm2_g'][l], p['peer_wq'][l], kbd, u_b, vt_b, tn_sel=256, tn=512, ec=512)
    return x, (kn.reshape(bd, 1, N_AH, 2 * HD), v_new.reshape(bd, 1, N_AH, VD), jnp.swapaxes(conv_t, 0, 1),
               h_new, s1)


def kernel(x_prompt, x_sample, cache_k, cache_v, state_conv, state_lru, state_hgrn, page_table,
           norm1_g, norm2_g, w_in, qk_norm_g, diff_lambda, subln_g, conv_w, conv_b,
           rg_wa, rg_ba, rg_wx, rg_bx, rg_lambda, hgrn_lb, hgrn_norm_g,
           w_br_attn, w_br_lru, w_br_hgrn, w_out, peer_wq, peer_keys, peer_u, peer_v):
    p = dict(norm1_g=norm1_g, norm2_g=norm2_g, w_in=w_in, qk_norm_g=qk_norm_g, diff_lambda=diff_lambda,
             subln_g=subln_g, conv_w=conv_w, conv_b=conv_b, rg_wa=rg_wa, rg_ba=rg_ba, rg_wx=rg_wx,
             rg_bx=rg_bx, rg_lambda=rg_lambda, hgrn_lb=hgrn_lb, hgrn_norm_g=hgrn_norm_g,
             w_br_attn=w_br_attn, w_br_lru=w_br_lru, w_br_hgrn=w_br_hgrn, w_out=w_out,
             peer_wq=peer_wq, peer_keys=peer_keys, peer_u=peer_u, peer_v=peer_v)
    b, s, _ = x_prompt.shape
    bd = x_sample.shape[0]
    xp = x_prompt.reshape(b * s, D_MODEL)
    xs = x_sample.reshape(bd, D_MODEL)
    st_p, st_s = [], []
    for l in range(DEPTH):
        w_in_b = w_in[l].astype(BF16)
        kbd = _peer_key_blocks(peer_keys[l])
        u_b = peer_u[l].astype(BF16)
        vt_b = peer_v[l].T.astype(BF16)
        xp, sp = layer_prompt(xp, l, p, w_in_b, kbd, u_b, vt_b, b=b, s=s)
        xs, ss = layer_sample(xs, l, p, cache_k, cache_v, page_table, state_conv, state_lru, state_hgrn,
                              kbd, u_b, vt_b)
        st_p.append(sp)
        st_s.append(ss)
    k_p, v_p, conv_p, lru_p, hgrn_p = [jnp.stack(a) for a in zip(*st_p)]
    k_s, v_s, conv_s, lru_s, hgrn_s = [jnp.stack(a) for a in zip(*st_s)]
    return (xp.reshape(b, s, D_MODEL), xs.reshape(bd, 1, D_MODEL), k_p, v_p, conv_p, lru_p, hgrn_p,
            k_s, v_s, conv_s, lru_s, hgrn_s)
```

```python
import functools
import math

import jax
import jax.numpy as jnp
from jax import lax
from jax.experimental import pallas as pl
from jax.experimental.pallas import tpu as pltpu

F32 = jnp.float32
BF16 = jnp.bfloat16

D_MODEL = 1024
DEPTH = 2
PAGE_SIZE = 128
N_AH = 8
HD = 64
VD = 2 * HD
ATTN_W = N_AH * VD
LRU_W = 1024
N_LB = 8
LB = LRU_W // N_LB
CONV_W = 4
LRU_C = 8.0
N_HH = 8
HK = 128
HV = 128
HGRN_W = N_HH * HV
P_HEADS = 8
N_KEYS = 128
N_EXP = N_KEYS * N_KEYS
D_KEY = 128
P_TOPK = 16
EPS = 1e-6
IN_W = 12 * D_MODEL

COL_Q, COL_K, COL_V, COL_LX, COL_LG, COL_HQ, COL_HF, COL_HI, COL_HG, COL_GT = 0, 1, 2, 3, 4, 5, 6, 7, 8, 9

LANES = 128
SUBLANES = 8
VMEM_LIMIT = 56 * 1024 * 1024

NEG = -1e30
SQRT_HALF = 0.7071067811865476


def _cparams(sem, flags=None):
    return pltpu.CompilerParams(dimension_semantics=sem, vmem_limit_bytes=VMEM_LIMIT, flags=flags)


def _split(a):
    hi = a.astype(BF16)
    lo = (a - hi.astype(F32)).astype(BF16)
    return hi, lo


def _dot(a, b):
    return jnp.dot(a, b, preferred_element_type=F32)


def _dot_nt(a, b):
    return lax.dot_general(a, b, (((1,), (1,)), ((), ())), preferred_element_type=F32)


def _dot3(a, b):
    ah, al = _split(a)
    bh, bl = _split(b)
    return _dot(ah, bh) + _dot(al, bh) + _dot(ah, bl)


def _dot3_nt(a, b):
    ah, al = _split(a)
    bh, bl = _split(b)
    return _dot_nt(ah, bh) + _dot_nt(al, bh) + _dot_nt(ah, bl)


def _dot2_exact_rhs(a, b_bf16):
    ah, al = _split(a)
    return _dot(ah, b_bf16) + _dot(al, b_bf16)


def _sigmoid(x):
    return 1.0 / (1.0 + jnp.exp(-x))


def _gelu(x):
    return 0.5 * x * (1.0 + lax.erf(x * SQRT_HALF))


def _silu(x):
    return x * _sigmoid(x)


def _softplus(x):
    return jnp.maximum(x, 0.0) + jnp.log1p(jnp.exp(-jnp.abs(x)))


def _expm1(x):
    u = jnp.exp(x)
    um1 = u - 1.0
    corrected = um1 * x / jnp.log(u)
    return jnp.where(um1 == 0.0, x, jnp.where(um1 == -1.0, -1.0, corrected))


def _rms(x, g):
    ms = jnp.mean(x * x, axis=-1, keepdims=True)
    return x * lax.rsqrt(ms + EPS) * g


def _rms_mm_kernel(x_ref, g_ref, w_ref, o_ref, xn_ref, *, exact):
    @pl.when(pl.program_id(1) == 0)
    def _():
        xn_ref[...] = _rms(x_ref[...], g_ref[...]).astype(xn_ref.dtype)

    if exact:
        o_ref[...] = _dot3(xn_ref[...], w_ref[...])
    else:
        o_ref[...] = _dot(xn_ref[...], w_ref[...])


def rms_matmul(x, g, w, *, exact, tm, tn):
    n, d = x.shape
    wd = w.shape[1]
    tm = min(tm, n)
    tn = min(tn, wd)
    return pl.pallas_call(
        functools.partial(_rms_mm_kernel, exact=exact),
        grid=(n // tm, wd // tn),
        in_specs=[
            pl.BlockSpec((tm, d), lambda i, j: (i, 0)),
            pl.BlockSpec((1, d), lambda i, j: (0, 0)),
            pl.BlockSpec((d, tn), lambda i, j: (0, j)),
        ],
        out_specs=pl.BlockSpec((tm, tn), lambda i, j: (i, j)),
        out_shape=jax.ShapeDtypeStruct((n, wd), F32),
        scratch_shapes=[pltpu.VMEM((tm, d), F32 if exact else BF16)],
        compiler_params=_cparams(("parallel", "arbitrary")),
        name="rms_matmul",
    )(x, g.reshape(1, d), w)


def _seg_matrix():
    r = lax.broadcasted_iota(jnp.int32, (LANES, LANES), 0) // HD
    c = lax.broadcasted_iota(jnp.int32, (LANES, LANES), 1) // HD
    return jnp.where(r == c, 1.0, 0.0).astype(BF16)


def _subhead_norm(x, g, seg):
    ss = _dot2_exact_rhs(x * x, seg)
    return x * lax.rsqrt(ss * (1.0 / HD) + EPS) * g


def _qk_norm_kernel(q_ref, k_ref, v_ref, g_ref, qb_ref, kn_ref, kb_ref, vb_ref):
    seg = _seg_matrix()
    gq = g_ref[0:1, :]
    gk = g_ref[1:2, :]
    for h in range(N_AH):
        sl = slice(h * LANES, (h + 1) * LANES)
        qn = _subhead_norm(q_ref[:, sl], gq, seg)
        kn = _subhead_norm(k_ref[:, sl], gk, seg)
        qb_ref[:, sl] = (qn * (HD ** -0.5)).astype(qb_ref.dtype)
        kn_ref[:, sl] = kn
        kb_ref[:, sl] = kn.astype(BF16)
    vb_ref[...] = v_ref[...].astype(BF16)


def qk_norm(y, g2, *, tm, q_dtype=BF16):
    n = y.shape[0]
    tm = min(tm, n)
    w = ATTN_W
    col = lambda c: pl.BlockSpec((tm, w), lambda i: (i, c))
    out = pl.BlockSpec((tm, w), lambda i: (i, 0))
    return pl.pallas_call(
        _qk_norm_kernel,
        grid=(n // tm,),
        in_specs=[col(COL_Q), col(COL_K), col(COL_V), pl.BlockSpec((2, LANES), lambda i: (0, 0))],
        out_specs=[out, out, out, out],
        out_shape=[
            jax.ShapeDtypeStruct((n, w), q_dtype),
            jax.ShapeDtypeStruct((n, w), F32),
            jax.ShapeDtypeStruct((n, w), BF16),
            jax.ShapeDtypeStruct((n, w), BF16),
        ],
        compiler_params=_cparams(("parallel",)),
        name="qk_norm",
    )(y, y, y, g2)


def _diff_lambda(lv, lam_init):
    t1 = jnp.sum(lv[0:1, :] * lv[1:2, :], axis=-1, keepdims=True)
    t2 = jnp.sum(lv[2:3, :] * lv[3:4, :], axis=-1, keepdims=True)
    return jnp.exp(t1) - jnp.exp(t2) + lam_init


def _attn_kernel(slope_ref, lv_ref, sg_ref, q_ref, k_ref, v_ref, o_ref, m_ref, acc_ref, *, tq, lam_init):
    h = pl.program_id(1)
    qi = pl.program_id(2)
    slope = slope_ref[h]
    q = q_ref[...]
    first = lax.broadcasted_iota(jnp.int32, (tq, LANES), 1) < HD
    col = lax.broadcasted_iota(jnp.int32, (1, tq), 1)
    ones = jnp.ones((tq, LANES), BF16)
    m_ref[...] = jnp.full(m_ref.shape, NEG, F32)
    acc_ref[...] = jnp.zeros_like(acc_ref)

    def update(c, s, v1):
        m_old = m_ref[c]
        m_new = jnp.maximum(m_old, jnp.max(s, axis=-1, keepdims=True))
        p = jnp.exp(s - m_new).astype(BF16)
        acc_ref[c] = acc_ref[c] * jnp.exp(m_old - m_new) + _dot(p, v1)
        m_ref[c] = m_new

    def block(j, masked):
        start = pl.multiple_of(j * tq, tq)
        kk = k_ref[pl.ds(start, tq), :]
        v1 = jnp.concatenate([v_ref[pl.ds(start, tq), :], ones], axis=-1)
        zero = jnp.zeros_like(kk)
        bias = slope * ((j - qi) * tq + col).astype(F32)
        for c in range(2):
            kc = jnp.where(first, kk, zero) if c == 0 else jnp.where(first, zero, kk)
            s = _dot_nt(q, kc) + bias
            if masked:
                row2 = lax.broadcasted_iota(jnp.int32, (tq, tq), 0)
                col2 = lax.broadcasted_iota(jnp.int32, (tq, tq), 1)
                s = jnp.where(col2 <= row2, s, NEG)
            update(c, s, v1)

    def body(j, carry):
        block(j, False)
        return carry

    lax.fori_loop(0, qi, body, 0)
    block(qi, True)

    lam = _diff_lambda(lv_ref[...], lam_init)
    r0 = acc_ref[0]
    r1 = acc_ref[1]
    o = r0[:, :LANES] / r0[:, LANES:] - lam * (r1[:, :LANES] / r1[:, LANES:])
    o_ref[...] = _rms(o, sg_ref[...]) * (1.0 - lam_init)


def attn_prompt(qb, kb, vb, lv, sg, slopes, *, b, s, lam_init, tq):
    n = b * s
    tq = min(tq, s)
    nq = s // tq
    return pl.pallas_call(
        functools.partial(_attn_kernel, tq=tq, lam_init=lam_init),
        grid=(b, N_AH, nq),
        in_specs=[
            pl.BlockSpec(memory_space=pltpu.SMEM),
            pl.BlockSpec((4, HD), lambda bi, h, i: (0, 0)),
            pl.BlockSpec((1, VD), lambda bi, h, i: (0, 0)),
            pl.BlockSpec((tq, LANES), lambda bi, h, i: (bi * nq + i, h)),
            pl.BlockSpec((s, LANES), lambda bi, h, i: (bi, h)),
            pl.BlockSpec((s, LANES), lambda bi, h, i: (bi, h)),
        ],
        out_specs=pl.BlockSpec((tq, LANES), lambda bi, h, i: (bi * nq + i, h)),
        out_shape=jax.ShapeDtypeStruct((n, ATTN_W), F32),
        scratch_shapes=[pltpu.VMEM((2, tq, 1), F32), pltpu.VMEM((2, tq, 2 * LANES), F32)],
        compiler_params=_cparams(("parallel", "parallel", "arbitrary")),
        name="attn_prompt",
    )(slopes, lv, sg.reshape(1, VD), qb, kb, vb)


def _scan_rows(a, u):
    tt = a.shape[0]
    row = lax.broadcasted_iota(jnp.int32, a.shape, 0)
    s = 1
    while s < tt:
        keep = row >= s
        a_prev = jnp.where(keep, pltpu.roll(a, s, 0), 1.0)
        u_prev = jnp.where(keep, pltpu.roll(u, s, 0), 0.0)
        u = a * u_prev + u
        a = a * a_prev
        s *= 2
    return a, u


def _lru_gates(xc, wa_ref, ba, wx_ref, bx, exact):
    rs, is_ = [], []
    for nb in range(N_LB):
        xs = xc[:, nb * LB:(nb + 1) * LB]
        if exact:
            rs.append(_dot3(xs, wa_ref[nb]))
            is_.append(_dot3(xs, wx_ref[nb]))
        else:
            xb = xs.astype(BF16)
            rs.append(_dot(xb, wa_ref[nb]))
            is_.append(_dot(xb, wx_ref[nb]))
    r = _sigmoid(jnp.concatenate(rs, axis=-1) + ba)
    i = _sigmoid(jnp.concatenate(is_, axis=-1) + bx)
    return r, i


def _lru_au(xc, r, i, lam):
    log_a = -LRU_C * r * _softplus(-lam)
    a = jnp.exp(log_a)
    u = jnp.sqrt(-_expm1(2.0 * log_a)) * (i * xc)
    return a, u


def _lru_kernel(lx_ref, lg_ref, cw_ref, cb_ref, wa_ref, ba_ref, wx_ref, bx_ref, lam_ref,
                y_ref, conv_ref, hl_ref, xbuf, hc, *, tt):
    t = pl.program_id(1)
    pad = SUBLANES

    @pl.when(t == 0)
    def _():
        xbuf[0:pad, :] = jnp.zeros((pad, LRU_W), F32)
        hc[...] = jnp.zeros_like(hc)

    x = lx_ref[...]
    xbuf[pad:pad + tt, :] = x
    cw = cw_ref[...]
    xc = cb_ref[...] + cw[3:4, :] * x
    for j in range(CONV_W - 1):
        back = CONV_W - 1 - j
        xc = xc + cw[j:j + 1, :] * xbuf[pad - back:pad - back + tt, :]
    xbuf[0:pad, :] = x[tt - pad:tt, :]

    r, i = _lru_gates(xc, wa_ref, ba_ref[...], wx_ref, bx_ref[...], False)
    a, u = _lru_au(xc, r, i, lam_ref[...])
    ca, cu = _scan_rows(a, u)
    hseq = ca * hc[0:1, :] + cu
    hc[0:1, :] = hseq[tt - 1:tt, :]
    y_ref[...] = (hseq * _gelu(lg_ref[...])).astype(y_ref.dtype)

    @pl.when(t == pl.num_programs(1) - 1)
    def _():
        conv_ref[...] = x[tt - (CONV_W - 1):tt, :]
        hl_ref[...] = hseq[tt - 1:tt, :]


def lru_prompt(y, cw, cb, wa, ba, wx, bx, lam, *, b, s, tt):
    n = b * s
    tt = min(tt, s)
    nt = s // tt
    vec = lambda: pl.BlockSpec((1, LRU_W), lambda bi, t: (0, 0))
    mat = lambda: pl.BlockSpec((N_LB, LB, LB), lambda bi, t: (0, 0, 0))
    return pl.pallas_call(
        functools.partial(_lru_kernel, tt=tt),
        grid=(b, nt),
        in_specs=[
            pl.BlockSpec((tt, LRU_W), lambda bi, t: (bi * nt + t, COL_LX)),
            pl.BlockSpec((tt, LRU_W), lambda bi, t: (bi * nt + t, COL_LG)),
            pl.BlockSpec((CONV_W, LRU_W), lambda bi, t: (0, 0)),
            vec(), mat(), vec(), mat(), vec(), vec(),
        ],
        out_specs=[
            pl.BlockSpec((tt, LRU_W), lambda bi, t: (bi * nt + t, 0)),
            pl.BlockSpec((None, CONV_W - 1, LRU_W), lambda bi, t: (bi, 0, 0)),
            pl.BlockSpec((None, 1, LRU_W), lambda bi, t: (bi, 0, 0)),
        ],
        out_shape=[
            jax.ShapeDtypeStruct((n, LRU_W), BF16),
            jax.ShapeDtypeStruct((b, CONV_W - 1, LRU_W), F32),
            jax.ShapeDtypeStruct((b, 1, LRU_W), F32),
        ],
        scratch_shapes=[pltpu.VMEM((tt + SUBLANES, LRU_W), F32), pltpu.VMEM((SUBLANES, LRU_W), F32)],
        compiler_params=_cparams(("parallel", "arbitrary")),
        name="lru_prompt",
    )(y, y, cw, cb.reshape(1, -1), wa.astype(BF16), ba.reshape(1, -1), wx.astype(BF16),
      bx.reshape(1, -1), lam.reshape(1, -1))


H_CHUNK = 64
H_SUB = SUBLANES


def _hgrn_lower_bound(rows, layer):
    mx = rows[0]
    for r in rows[1:]:
        mx = jnp.maximum(mx, r)
    es = [jnp.exp(r - mx) for r in rows]
    tot = es[0]
    for e in es[1:]:
        tot = tot + e
    lb = jnp.zeros_like(mx)
    for d in range(1, layer + 1):
        lb = lb + es[d] / tot
    return lb


def _hgrn_log_f(z, lb):
    log_sig = jnp.minimum(z, 0.0) - jnp.log1p(jnp.exp(-jnp.abs(z)))
    a = jnp.log(lb)
    bb = jnp.log1p(-lb) + log_sig
    return jnp.maximum(a, bb) + jnp.log1p(jnp.exp(-jnp.abs(a - bb)))


def _cumsum_rows(x):
    n = x.shape[0]
    row = lax.broadcasted_iota(jnp.int32, x.shape, 0)
    s = 1
    while s < n:
        x = x + jnp.where(row >= s, pltpu.roll(x, s, 0), 0.0)
        s *= 2
    return x


def _hgrn_chunk(qq, kk, logf, vv, st):
    c = qq.shape[0]
    nsub = c // H_SUB
    bcum = _cumsum_rows(logf)
    row_c = lax.broadcasted_iota(jnp.int32, (c, LANES), 0)
    row_s = lax.broadcasted_iota(jnp.int32, (H_SUB, LANES), 0)
    vb = vv.astype(BF16)

    o = _dot_nt((qq * jnp.exp(bcum)).astype(BF16), st.astype(BF16))

    att_rows = []
    diag_rows = []
    for i in range(nsub):
        r0 = i * H_SUB
        q_r = qq[r0:r0 + H_SUB, :]
        b_r = bcum[r0:r0 + H_SUB, :]
        if i == 0:
            att_rows.append(jnp.zeros((H_SUB, c), F32))
        else:
            beta = bcum[r0 - 1:r0, :]
            qt = q_r * jnp.exp(b_r - beta)
            kt = kk * jnp.exp(jnp.where(row_c < r0, beta - bcum, NEG))
            att_rows.append(_dot_nt(qt.astype(BF16), kt.astype(BF16)))
        k_r = kk[r0:r0 + H_SUB, :]
        v_r = vv[r0:r0 + H_SUB, :]
        od = jnp.zeros((H_SUB, LANES), F32)
        for s in range(H_SUB):
            e = jnp.exp(jnp.where(row_s >= s, b_r - b_r[s:s + 1, :], NEG))
            w = jnp.sum(q_r * k_r[s:s + 1, :] * e, axis=-1, keepdims=True)
            od = od + w * v_r[s:s + 1, :]
        diag_rows.append(od)
    att = jnp.concatenate(att_rows, axis=0)
    o = o + _dot(att.astype(BF16), vb) + jnp.concatenate(diag_rows, axis=0)

    b_last = bcum[c - 1:c, :]
    kd = (kk * jnp.exp(b_last - bcum)).astype(BF16)
    st = st * jnp.exp(b_last) + lax.dot_general(vb, kd, (((0,), (0,)), ((), ())),
                                                 preferred_element_type=F32)
    return o, st


def _hgrn_kernel(hq_ref, hf_ref, hi_ref, hg_ref, lb_ref, g_ref, y_ref, s_ref, st_ref, *, layer, nchunk):
    lb = _hgrn_lower_bound([lb_ref[d:d + 1, :] for d in range(DEPTH)], layer)
    g = g_ref[...]
    st_ref[...] = jnp.zeros_like(st_ref)

    def body(ci, carry):
        r0 = pl.multiple_of(ci * H_CHUNK, H_CHUNK)
        rows = pl.ds(r0, H_CHUNK)
        z = hf_ref[rows, :]
        logf = _hgrn_log_f(z, lb)
        kk = (1.0 - lb) * _sigmoid(-z)
        qq = _silu(hq_ref[rows, :])
        o, st = _hgrn_chunk(qq, kk, logf, hi_ref[rows, :], st_ref[...])
        st_ref[...] = st
        y_ref[rows, :] = (_rms(o, g) * _silu(hg_ref[rows, :])).astype(y_ref.dtype)
        return carry

    lax.fori_loop(0, nchunk, body, 0, unroll=4)
    s_ref[...] = st_ref[...]


def hgrn_prompt(y, lb_all, g, *, b, s, layer):
    n = b * s
    nchunk = s // H_CHUNK
    col = lambda c: pl.BlockSpec((s, LANES), lambda bi, h: (bi, c * N_HH + h))
    return pl.pallas_call(
        functools.partial(_hgrn_kernel, layer=layer, nchunk=nchunk),
        grid=(b, N_HH),
        in_specs=[
            col(COL_HQ), col(COL_HF), col(COL_HI), col(COL_HG),
            pl.BlockSpec((DEPTH, LANES), lambda bi, h: (0, h)),
            pl.BlockSpec((1, HV), lambda bi, h: (0, 0)),
        ],
        out_specs=[
            pl.BlockSpec((s, LANES), lambda bi, h: (bi, h)),
            pl.BlockSpec((None, None, HV, HK), lambda bi, h: (bi, h, 0, 0)),
        ],
        out_shape=[
            jax.ShapeDtypeStruct((n, HGRN_W), BF16),
            jax.ShapeDtypeStruct((b, N_HH, HV, HK), F32),
        ],
        scratch_shapes=[pltpu.VMEM((HV, HK), F32)],
        compiler_params=_cparams(("parallel", "parallel")),
        name="hgrn_prompt",
    )(y, y, y, y, lb_all, g.reshape(1, HV))


def _merge_kernel(x_ref, ya_ref, yl_ref, yh_ref, ga_ref, gl_ref, gh_ref,
                  wa_ref, wl_ref, wh_ref, wo_ref, o_ref, *, exact):
    if exact:
        mm = _dot3
        cast = lambda a: a.astype(F32)
    else:
        mm = _dot
        cast = lambda a: a.astype(BF16)
    m = _sigmoid(ga_ref[...]) * mm(cast(ya_ref[...]), wa_ref[...])
    m = m + _sigmoid(gl_ref[...]) * mm(cast(yl_ref[...]), wl_ref[...])
    m = m + _sigmoid(gh_ref[...]) * mm(cast(yh_ref[...]), wh_ref[...])
    o_ref[...] = x_ref[...] + mm(cast(m), wo_ref[...])


def merge_out(x, y, ya, yl, yh, wa, wl, wh, wo, *, exact, tm):
    n = x.shape[0]
    tm = min(tm, n)
    d = D_MODEL
    row = lambda: pl.BlockSpec((tm, d), lambda i: (i, 0))
    gate = lambda c: pl.BlockSpec((tm, d), lambda i: (i, COL_GT + c))
    wspec = lambda: pl.BlockSpec((d, d), lambda i: (0, 0))
    wdt = F32 if exact else BF16
    return pl.pallas_call(
        functools.partial(_merge_kernel, exact=exact),
        grid=(n // tm,),
        in_specs=[row(), row(), row(), row(), gate(0), gate(1), gate(2), wspec(), wspec(), wspec(), wspec()],
        out_specs=row(),
        out_shape=jax.ShapeDtypeStruct((n, d), F32),
        compiler_params=_cparams(("parallel",)),
        name="merge_out",
    )(x, ya, yl, yh, y, y, y, wa.astype(wdt), wl.astype(wdt), wh.astype(wdt), wo.astype(wdt))


def _topk_rows(x, k):
    out = []
    cur = x
    for _ in range(k):
        mx = jnp.max(cur, axis=0, keepdims=True)
        out.append(mx)
        cur = jnp.where(cur == mx, NEG, cur)
    return out


def _peer_select_kernel(q_ref, kh_ref, sel_ref):
    st = _dot3_nt(kh_ref[...], q_ref[...])
    s1 = st[:N_KEYS, :]
    s2 = st[N_KEYS:, :]
    v1 = _topk_rows(s1, P_TOPK)
    v2 = _topk_rows(s2, P_TOPK)
    v2m = jnp.concatenate(v2, axis=0)
    cands = [v1[a] + v2m for a in range(P_TOPK)]

    cur = cands
    tau = None
    for _ in range(P_TOPK):
        mx = cur[0]
        for c in cur[1:]:
            mx = jnp.maximum(mx, c)
        tau = jnp.max(mx, axis=0, keepdims=True)
        cur = [jnp.where(c == tau, NEG, c) for c in cur]

    top = v1[0] + v2[0]
    z = jnp.zeros_like(tau)
    theta_dense = jnp.full(s1.shape, -NEG, F32)
    for a in range(P_TOPK):
        sel = cands[a] >= tau
        z = z + jnp.sum(jnp.where(sel, jnp.exp(cands[a] - top), 0.0), axis=0, keepdims=True)
        theta_a = jnp.min(jnp.where(sel, v2m, -NEG), axis=0, keepdims=True)
        theta_dense = jnp.where(s1 == v1[a], theta_a, theta_dense)
    sel_ref[0] = s2
    sel_ref[1] = jnp.exp(s2 - v2[0])
    sel_ref[2] = theta_dense
    sel_ref[3] = jnp.exp(s1 - v1[0]) / z


def peer_select(q, kbd, *, tn):
    n = q.shape[0]
    tn = min(tn, n)
    return pl.pallas_call(
        _peer_select_kernel,
        grid=(n // tn, P_HEADS),
        in_specs=[
            pl.BlockSpec((tn, D_KEY), lambda i, h: (i, h)),
            pl.BlockSpec((None, 2 * N_KEYS, D_KEY), lambda i, h: (h, 0, 0)),
        ],
        out_specs=pl.BlockSpec((None, 4, N_KEYS, tn), lambda i, h: (h, 0, 0, i)),
        out_shape=jax.ShapeDtypeStruct((P_HEADS, 4, N_KEYS, n), F32),
        compiler_params=_cparams(("parallel", "parallel")),
        name="peer_select",
    )(q, kbd)


def _peer_dense_kernel(xn_ref, x_ref, sel_ref, u_ref, vt_ref, o_ref, acc_ref, w_ref, sp_ref, bc_ref, *, ec):
    e = pl.program_id(1)
    tn = w_ref.shape[1]
    n_i1 = ec // N_KEYS
    nt = tn // LANES
    tile = lambda t: slice((t % nt) * LANES, (t % nt + 1) * LANES)
    rot_sp = lambda h, k: h + 2 * k
    rot_bc = lambda ii, h, k: h + 2 * k + ii + 1

    @pl.when(e == 0)
    def _():
        acc_ref[...] = jnp.zeros_like(acc_ref)
        for h in range(P_HEADS):
            for k in range(2):
                for t in range(nt):
                    sp_ref[h, k, :, tile(t + rot_sp(h, k))] = sel_ref[h, k, :, tile(t)]

    for ii in range(n_i1):
        i1 = e * n_i1 + ii
        for h in range(P_HEADS):
            for k in range(2):
                row = sel_ref[h, 2 + k, pl.ds(i1, 1), :]
                for t in range(nt):
                    bc_ref[ii, h, k, :, tile(t + rot_bc(ii, h, k))] = jnp.broadcast_to(row[:, tile(t)],
                                                                                      (SUBLANES, LANES))

    def rows_body(r, carry):
        r0 = pl.multiple_of(r * SUBLANES, SUBLANES)
        rows = pl.ds(r0, SUBLANES)
        w = [[None] * n_i1 for _ in range(nt)]
        for h in range(P_HEADS):
            s2 = [sp_ref[h, 0, rows, tile(t + rot_sp(h, 0))] for t in range(nt)]
            p2 = [sp_ref[h, 1, rows, tile(t + rot_sp(h, 1))] for t in range(nt)]
            for ii in range(n_i1):
                for t in range(nt):
                    theta = bc_ref[ii, h, 0, :, tile(t + rot_bc(ii, h, 0))]
                    c1 = bc_ref[ii, h, 1, :, tile(t + rot_bc(ii, h, 1))]
                    term = jnp.where(s2[t] >= theta, p2[t], 0.0) * c1
                    w[t][ii] = term if w[t][ii] is None else w[t][ii] + term
        for t in range(nt):
            for ii in range(n_i1):
                w_ref[pl.ds(ii * N_KEYS + r0, SUBLANES), tile(t)] = w[t][ii]
        return carry

    lax.fori_loop(0, N_KEYS // SUBLANES, rows_body, 0)
    a_t = _dot_nt(u_ref[...], xn_ref[...])
    h_t = (w_ref[...] * _gelu(a_t)).astype(BF16)
    acc_ref[...] += _dot(vt_ref[...], h_t)

    @pl.when(e == pl.num_programs(1) - 1)
    def _():
        o_ref[...] = x_ref[...] + acc_ref[...].T


def peer_dense(xn, x, sel, u, vt, *, tn, ec):
    n = x.shape[0]
    tn = min(tn, n)
    d = D_MODEL
    return pl.pallas_call(
        functools.partial(_peer_dense_kernel, ec=ec),
        grid=(n // tn, N_EXP // ec),
        in_specs=[
            pl.BlockSpec((tn, d), lambda i, e: (i, 0)),
            pl.BlockSpec((tn, d), lambda i, e: (i, 0)),
            pl.BlockSpec((P_HEADS, 4, N_KEYS, tn), lambda i, e: (0, 0, 0, i)),
            pl.BlockSpec((ec, d), lambda i, e: (e, 0)),
            pl.BlockSpec((d, ec), lambda i, e: (0, e)),
        ],
        out_specs=pl.BlockSpec((tn, d), lambda i, e: (i, 0)),
        out_shape=jax.ShapeDtypeStruct((n, d), F32),
        scratch_shapes=[pltpu.VMEM((d, tn), F32), pltpu.VMEM((ec, tn), F32),
                        pltpu.VMEM((P_HEADS, 2, N_KEYS, tn), F32),
                        pltpu.VMEM((ec // N_KEYS, P_HEADS, 2, SUBLANES, tn), F32)],
        compiler_params=_cparams(("parallel", "arbitrary")),
        name="peer_dense",
    )(xn, x, sel, u, vt)


def _rms_cast_kernel(x_ref, g_ref, o_ref):
    o_ref[...] = _rms(x_ref[...], g_ref[...]).astype(o_ref.dtype)


def rms_cast(x, g, *, tm):
    n, d = x.shape
    tm = min(tm, n)
    return pl.pallas_call(
        _rms_cast_kernel,
        grid=(n // tm,),
        in_specs=[pl.BlockSpec((tm, d), lambda i: (i, 0)), pl.BlockSpec((1, d), lambda i: (0, 0))],
        out_specs=pl.BlockSpec((tm, d), lambda i: (i, 0)),
        out_shape=jax.ShapeDtypeStruct((n, d), BF16),
        compiler_params=_cparams(("parallel",)),
        name="rms_cast",
    )(x, g.reshape(1, d))


def _peer_key_blocks(keys):
    z = jnp.zeros_like(keys[:, 0])
    top = jnp.concatenate([keys[:, 0], z], axis=-1)
    bot = jnp.concatenate([z, keys[:, 1]], axis=-1)
    return jnp.concatenate([top, bot], axis=1)


def peer_ffn(x, g, wq, kbd, u_b, vt_b, *, tn_sel, tn, ec):
    n = x.shape[0]
    pad = (-n) % LANES
    xp = jnp.pad(x, ((0, pad), (0, 0))) if pad else x
    q = rms_matmul(xp, g, wq, exact=True, tm=512, tn=1024)
    sel = peer_select(q, kbd, tn=tn_sel)
    xn = rms_cast(xp, g, tm=512)
    out = peer_dense(xn, xp, sel, u_b, vt_b, tn=tn, ec=ec)
    return out[:n] if pad else out


def _lam_init(layer):
    return 0.8 - 0.6 * math.exp(-0.3 * layer)


def _alibi_slopes():
    return jnp.exp2(-8.0 * jnp.arange(1, N_AH + 1, dtype=F32) / N_AH)


def layer_prompt(x, l, p, w_in_b, kbd, u_b, vt_b, *, b, s):
    n = b * s
    y = rms_matmul(x, p['norm1_g'][l], w_in_b, exact=False, tm=1024, tn=1024)
    g2 = jnp.tile(p['qk_norm_g'][l], (1, 2))
    qb, kn, kb, vb = qk_norm(y, g2, tm=512)
    ya = attn_prompt(qb, kb, vb, p['diff_lambda'][l], p['subln_g'][l], _alibi_slopes(),
                     b=b, s=s, lam_init=_lam_init(l), tq=1024)
    yl, conv_new, h_last = lru_prompt(y, p['conv_w'][l], p['conv_b'][l], p['rg_wa'][l], p['rg_ba'][l],
                                      p['rg_wx'][l], p['rg_bx'][l], p['rg_lambda'][l], b=b, s=s, tt=256)
    yh, st = hgrn_prompt(y, p['hgrn_lb'], p['hgrn_norm_g'][l], b=b, s=s, layer=l)
    x = merge_out(x, y, ya, yl, yh, p['w_br_attn'][l], p['w_br_lru'][l], p['w_br_hgrn'][l],
                  p['w_out'][l], exact=False, tm=512)
    x = peer_ffn(x, p['norm2_g'][l], p['peer_wq'][l], kbd, u_b, vt_b, tn_sel=256, tn=512, ec=512)
    k_out = kn.reshape(b, s, N_AH, 2 * HD)
    v_out = y[:, COL_V * D_MODEL:(COL_V + 1) * D_MODEL].reshape(b, s, N_AH, VD)
    s_out = jnp.swapaxes(st, -1, -2)
    return x, (k_out, v_out, conv_new, h_last.reshape(b, LRU_W), s_out)


DEC_PPS = 4


def _swap_halves(a):
    return pltpu.roll(a, HD, a.ndim - 1)


def _attn_decode_kernel(pt_ref, q_ref, kn_ref, vn_ref, slope_ref, lv_ref, sg_ref, *refs,
                        pps, past, lam_init):
    k_refs = refs[:pps]
    v_refs = refs[pps:2 * pps]
    o_ref, m_ref, l_ref, acc_ref = refs[2 * pps:]
    step = pl.program_id(1)

    @pl.when(step == 0)
    def _():
        m_ref[...] = jnp.full(m_ref.shape, NEG, F32)
        l_ref[...] = jnp.zeros_like(l_ref)
        acc_ref[...] = jnp.zeros_like(acc_ref)

    q = q_ref[...]
    seg = _seg_matrix()
    slope = slope_ref[...]
    tbias = slope[None] * lax.broadcasted_iota(jnp.int32, (PAGE_SIZE, N_AH, LANES), 0).astype(F32)

    def accumulate(sc, vals, off, reduce):
        m_old = m_ref[...]
        m_new = jnp.maximum(m_old, (jnp.max(sc, axis=0) if reduce else sc) + off)
        alpha = jnp.exp(m_old - m_new)
        pw = jnp.exp(sc - (m_new - off))
        pw_x = _swap_halves(pw)
        red = (lambda a: jnp.sum(a, axis=0)) if reduce else (lambda a: a)
        l_ref[...] = l_ref[...] * alpha + red(pw)
        acc_ref[0] = acc_ref[0] * alpha + red(pw * vals)
        acc_ref[1] = acc_ref[1] * _swap_halves(alpha) + red(pw_x * vals)
        m_ref[...] = m_new

    for i in range(pps):
        kpage = k_refs[i][...]
        prod = (kpage * q[None]).reshape(PAGE_SIZE * N_AH, LANES).astype(BF16)
        s = _dot(prod, seg).reshape(PAGE_SIZE, N_AH, LANES)
        off = slope * ((step * pps + i) * PAGE_SIZE - past).astype(F32)
        accumulate(s + tbias, v_refs[i][...], off, True)

    @pl.when(step == pl.num_programs(1) - 1)
    def _():
        sn = _dot2_exact_rhs(kn_ref[...] * q, seg)
        accumulate(sn, vn_ref[...], 0.0, False)
        first = lax.broadcasted_iota(jnp.int32, (N_AH, LANES), 1) < HD
        l = l_ref[...]
        l_x = _swap_halves(l)
        a_s = acc_ref[0]
        a_x = acc_ref[1]
        o0 = jnp.where(first, a_s, a_x) / jnp.where(first, l, l_x)
        o1 = jnp.where(first, a_x, a_s) / jnp.where(first, l_x, l)
        lam = _diff_lambda(lv_ref[...], lam_init)
        o = o0 - lam * o1
        o_ref[...] = _rms(o, sg_ref[...]) * (1.0 - lam_init)


def attn_decode(page_table, q, kn, vn, cache_k, cache_v, lv, sg, slopes, *, layer, lam_init):
    bd, n_pages = page_table.shape
    pps = DEC_PPS
    past = n_pages * PAGE_SIZE
    tok = lambda: pl.BlockSpec((None, N_AH, LANES), lambda b, st, pt: (b, 0, 0))
    page = lambda i: pl.BlockSpec((None, None, PAGE_SIZE, N_AH, LANES),
                                  lambda b, st, pt: (layer, pt[b, st * pps + i], 0, 0, 0))
    grid_spec = pltpu.PrefetchScalarGridSpec(
        num_scalar_prefetch=1,
        grid=(bd, n_pages // pps),
        in_specs=[
            tok(), tok(), tok(),
            pl.BlockSpec((N_AH, LANES), lambda b, st, pt: (0, 0)),
            pl.BlockSpec((4, HD), lambda b, st, pt: (0, 0)),
            pl.BlockSpec((1, VD), lambda b, st, pt: (0, 0)),
        ] + [page(i) for i in range(pps)] + [page(i) for i in range(pps)],
        out_specs=pl.BlockSpec((None, N_AH, LANES), lambda b, st, pt: (b, 0, 0)),
        scratch_shapes=[pltpu.VMEM((N_AH, LANES), F32), pltpu.VMEM((N_AH, LANES), F32),
                        pltpu.VMEM((2, N_AH, LANES), F32)],
    )
    slope_tile = jnp.broadcast_to(slopes[:, None], (N_AH, LANES))
    return pl.pallas_call(
        functools.partial(_attn_decode_kernel, pps=pps, past=past, lam_init=lam_init),
        grid_spec=grid_spec,
        out_shape=jax.ShapeDtypeStruct((bd, N_AH, LANES), F32),
        compiler_params=_cparams(("parallel", "arbitrary")),
        name="attn_decode",
    )(page_table, q, kn, vn, slope_tile, lv, sg.reshape(1, VD), *([cache_k] * pps), *([cache_v] * pps))


def _lru_step_kernel(lx_ref, lg_ref, buf_ref, h0_ref, cw_ref, cb_ref, wa_ref, ba_ref, wx_ref, bx_ref,
                     lam_ref, yl_ref, conv_ref, h_ref):
    x = lx_ref[...]
    cw = cw_ref[...]
    xc = cb_ref[...] + cw[CONV_W - 1:CONV_W, :] * x
    for j in range(CONV_W - 1):
        xc = xc + cw[j:j + 1, :] * buf_ref[j]
    r, i = _lru_gates(xc, wa_ref, ba_ref[...], wx_ref, bx_ref[...], True)
    a, u = _lru_au(xc, r, i, lam_ref[...])
    h = a * h0_ref[...] + u
    yl_ref[...] = h * _gelu(lg_ref[...])
    for j in range(CONV_W - 2):
        conv_ref[j] = buf_ref[j + 1]
    conv_ref[CONV_W - 2] = x
    h_ref[...] = h


def lru_step(y, buf_t, h0, cw, cb, wa, ba, wx, bx, lam):
    bd = y.shape[0]
    full = lambda shape: pl.BlockSpec(shape, lambda i: (0,) * len(shape))
    return pl.pallas_call(
        _lru_step_kernel,
        grid=(1,),
        in_specs=[
            pl.BlockSpec((bd, LRU_W), lambda i: (0, COL_LX)),
            pl.BlockSpec((bd, LRU_W), lambda i: (0, COL_LG)),
            full((CONV_W - 1, bd, LRU_W)), full((bd, LRU_W)), full((CONV_W, LRU_W)), full((1, LRU_W)),
            full((N_LB, LB, LB)), full((1, LRU_W)), full((N_LB, LB, LB)), full((1, LRU_W)), full((1, LRU_W)),
        ],
        out_specs=[full((bd, LRU_W)), full((CONV_W - 1, bd, LRU_W)), full((bd, LRU_W))],
        out_shape=[
            jax.ShapeDtypeStruct((bd, LRU_W), F32),
            jax.ShapeDtypeStruct((CONV_W - 1, bd, LRU_W), F32),
            jax.ShapeDtypeStruct((bd, LRU_W), F32),
        ],
        compiler_params=_cparams(("arbitrary",)),
        name="lru_step",
    )(y, y, buf_t, h0, cw, cb.reshape(1, -1), wa, ba.reshape(1, -1), wx, bx.reshape(1, -1), lam.reshape(1, -1))


def _hgrn_step_kernel(hq_ref, hf_ref, hi_ref, hg_ref, lb_ref, g_ref, s0_ref, y_ref, s1_ref, *, layer):
    lb = _hgrn_lower_bound([lb_ref[d] for d in range(DEPTH)], layer)
    z = hf_ref[...]
    f = jnp.exp(_hgrn_log_f(z, lb))
    kk = (1.0 - lb) * _sigmoid(-z)
    qq = _silu(hq_ref[...])
    vv = hi_ref[...]
    pad = jnp.zeros((HK - N_HH, LANES), F32)
    cols = lambda a: jnp.concatenate([a, pad], axis=0).T
    q_t, f_t, k_t = cols(qq), cols(f), cols(kk)
    row = lax.broadcasted_iota(jnp.int32, (N_HH, LANES), 0)
    o = jnp.zeros((N_HH, LANES), F32)
    for h in range(N_HH):
        s1 = s0_ref[h] * f_t[:, h:h + 1] + k_t[:, h:h + 1] * vv[h:h + 1, :]
        s1_ref[h] = s1
        oh = jnp.sum(q_t[:, h:h + 1] * s1, axis=0, keepdims=True)
        o = jnp.where(row == h, oh, o)
    y_ref[...] = _rms(o, g_ref[...]) * _silu(hg_ref[...])


def hgrn_step(y3, lb3, g, state, *, layer):
    bd = y3.shape[0]
    col = lambda c: pl.BlockSpec((None, N_HH, LANES), lambda b: (b, c, 0))
    return pl.pallas_call(
        functools.partial(_hgrn_step_kernel, layer=layer),
        grid=(bd,),
        in_specs=[
            col(COL_HQ), col(COL_HF), col(COL_HI), col(COL_HG),
            pl.BlockSpec((DEPTH, N_HH, LANES), lambda b: (0, 0, 0)),
            pl.BlockSpec((1, HV), lambda b: (0, 0)),
            pl.BlockSpec((None, None, N_HH, HK, HV), lambda b: (layer, b, 0, 0, 0)),
        ],
        out_specs=[
            pl.BlockSpec((None, N_HH, LANES), lambda b: (b, 0, 0)),
            pl.BlockSpec((None, N_HH, HK, HV), lambda b: (b, 0, 0, 0)),
        ],
        out_shape=[
            jax.ShapeDtypeStruct((bd, N_HH, LANES), F32),
            jax.ShapeDtypeStruct((bd, N_HH, HK, HV), F32),
        ],
        compiler_params=_cparams(("parallel",)),
        name="hgrn_step",
    )(y3, y3, y3, y3, lb3, g.reshape(1, HV), state)


def layer_sample(x, l, p, cache_k, cache_v, page_table, state_conv, state_lru, state_hgrn, kbd, u_b, vt_b):
    bd = x.shape[0]
    y = rms_matmul(x, p['norm1_g'][l], p['w_in'][l], exact=True, tm=bd, tn=1024)
    g2 = jnp.tile(p['qk_norm_g'][l], (1, 2))
    qf, kn, _, _ = qk_norm(y, g2, tm=bd, q_dtype=F32)
    v_new = y[:, COL_V * D_MODEL:(COL_V + 1) * D_MODEL]
    tok = lambda a: a.reshape(bd, N_AH, LANES)
    ya = attn_decode(page_table, tok(qf), tok(kn), tok(v_new), cache_k, cache_v, p['diff_lambda'][l],
                     p['subln_g'][l], _alibi_slopes(), layer=l, lam_init=_lam_init(l))
    yl, conv_t, h_new = lru_step(y, jnp.swapaxes(state_conv[l], 0, 1), state_lru[l], p['conv_w'][l],
                                 p['conv_b'][l], p['rg_wa'][l], p['rg_ba'][l], p['rg_wx'][l], p['rg_bx'][l],
                                 p['rg_lambda'][l])
    yh, s1 = hgrn_step(y.reshape(bd, IN_W // LANES, LANES), p['hgrn_lb'].reshape(DEPTH, N_HH, LANES),
                       p['hgrn_norm_g'][l], state_hgrn, layer=l)
    x = merge_out(x, y, ya.reshape(bd, ATTN_W), yl, yh.reshape(bd, HGRN_W), p['w_br_attn'][l],
                  p['w_br_lru'][l], p['w_br_hgrn'][l], p['w_out'][l], exact=True, tm=bd)
    x = peer_ffn(x, p['norm2_g'][l], p['peer_wq'][l], kbd, u_b, vt_b, tn_sel=256, tn=512, ec=512)
    return x, (kn.reshape(bd, 1, N_AH, 2 * HD), v_new.reshape(bd, 1, N_AH, VD), jnp.swapaxes(conv_t, 0, 1),
               h_new, s1)


def kernel(x_prompt, x_sample, cache_k, cache_v, state_conv, state_lru, state_hgrn, page_table,
           norm1_g, norm2_g, w_in, qk_norm_g, diff_lambda, subln_g, conv_w, conv_b,
           rg_wa, rg_ba, rg_wx, rg_bx, rg_lambda, hgrn_lb, hgrn_norm_g,
           w_br_attn, w_br_lru, w_br_hgrn, w_out, peer_wq, peer_keys, peer_u, peer_v):
    p = dict(norm1_g=norm1_g, norm2_g=norm2_g, w_in=w_in, qk_norm_g=qk_norm_g, diff_lambda=diff_lambda,
             subln_g=subln_g, conv_w=conv_w, conv_b=conv_b, rg_wa=rg_wa, rg_ba=rg_ba, rg_wx=rg_wx,
             rg_bx=rg_bx, rg_lambda=rg_lambda, hgrn_lb=hgrn_lb, hgrn_norm_g=hgrn_norm_g,
             w_br_attn=w_br_attn, w_br_lru=w_br_lru, w_br_hgrn=w_br_hgrn, w_out=w_out,
             peer_wq=peer_wq, peer_keys=peer_keys, peer_u=peer_u, peer_v=peer_v)
    b, s, _ = x_prompt.shape
    bd = x_sample.shape[0]
    xp = x_prompt.reshape(b * s, D_MODEL)
    xs = x_sample.reshape(bd, D_MODEL)
    st_p, st_s = [], []
    for l in range(DEPTH):
        w_in_b = w_in[l].astype(BF16)
        kbd = _peer_key_blocks(peer_keys[l])
        u_b = peer_u[l].astype(BF16)
        vt_b = peer_v[l].T.astype(BF16)
        xp, sp = layer_prompt(xp, l, p, w_in_b, kbd, u_b, vt_b, b=b, s=s)
        xs, ss = layer_sample(xs, l, p, cache_k, cache_v, page_table, state_conv, state_lru, state_hgrn,
                              kbd, u_b, vt_b)
        st_p.append(sp)
        st_s.append(ss)
    k_p, v_p, conv_p, lru_p, hgrn_p = [jnp.stack(a) for a in zip(*st_p)]
    k_s, v_s, conv_s, lru_s, hgrn_s = [jnp.stack(a) for a in zip(*st_s)]
    return (xp.reshape(b, s, D_MODEL), xs.reshape(bd, 1, D_MODEL), k_p, v_p, conv_p, lru_p, hgrn_p,
            k_s, v_s, conv_s, lru_s, hgrn_s)
```

```python
import functools
import math

import jax
import jax.numpy as jnp
from jax import lax
from jax.experimental import pallas as pl
from jax.experimental.pallas import tpu as pltpu

F32 = jnp.float32
BF16 = jnp.bfloat16

D_MODEL = 1024
DEPTH = 2
PAGE_SIZE = 128
N_AH = 8
HD = 64
VD = 2 * HD
ATTN_W = N_AH * VD
LRU_W = 1024
N_LB = 8
LB = LRU_W // N_LB
CONV_W = 4
LRU_C = 8.0
N_HH = 8
HK = 128
HV = 128
HGRN_W = N_HH * HV
P_HEADS = 8
N_KEYS = 128
N_EXP = N_KEYS * N_KEYS
D_KEY = 128
P_TOPK = 16
EPS = 1e-6
IN_W = 12 * D_MODEL

COL_Q, COL_K, COL_V, COL_LX, COL_LG, COL_HQ, COL_HF, COL_HI, COL_HG, COL_GT = 0, 1, 2, 3, 4, 5, 6, 7, 8, 9

LANES = 128
SUBLANES = 8
VMEM_LIMIT = 56 * 1024 * 1024

NEG = -1e30
SQRT_HALF = 0.7071067811865476


def _cparams(sem, flags=None):
    return pltpu.CompilerParams(dimension_semantics=sem, vmem_limit_bytes=VMEM_LIMIT, flags=flags)


def _split(a):
    hi = a.astype(BF16)
    lo = (a - hi.astype(F32)).astype(BF16)
    return hi, lo


def _dot(a, b):
    return jnp.dot(a, b, preferred_element_type=F32)


def _dot_nt(a, b):
    return lax.dot_general(a, b, (((1,), (1,)), ((), ())), preferred_element_type=F32)


def _dot3(a, b):
    ah, al = _split(a)
    bh, bl = _split(b)
    return _dot(ah, bh) + _dot(al, bh) + _dot(ah, bl)


def _dot3_nt(a, b):
    ah, al = _split(a)
    bh, bl = _split(b)
    return _dot_nt(ah, bh) + _dot_nt(al, bh) + _dot_nt(ah, bl)


def _dot2_exact_rhs(a, b_bf16):
    ah, al = _split(a)
    return _dot(ah, b_bf16) + _dot(al, b_bf16)


def _sigmoid(x):
    return 1.0 / (1.0 + jnp.exp(-x))


def _gelu(x):
    return 0.5 * x * (1.0 + lax.erf(x * SQRT_HALF))


def _silu(x):
    return x * _sigmoid(x)


def _softplus(x):
    return jnp.maximum(x, 0.0) + jnp.log1p(jnp.exp(-jnp.abs(x)))


def _expm1(x):
    u = jnp.exp(x)
    um1 = u - 1.0
    corrected = um1 * x / jnp.log(u)
    return jnp.where(um1 == 0.0, x, jnp.where(um1 == -1.0, -1.0, corrected))


def _rms(x, g):
    ms = jnp.mean(x * x, axis=-1, keepdims=True)
    return x * lax.rsqrt(ms + EPS) * g


def _rms_mm_kernel(x_ref, g_ref, w_ref, o_ref, xn_ref, *, exact):
    @pl.when(pl.program_id(1) == 0)
    def _():
        xn_ref[...] = _rms(x_ref[...], g_ref[...]).astype(xn_ref.dtype)

    if exact:
        o_ref[...] = _dot3(xn_ref[...], w_ref[...])
    else:
        o_ref[...] = _dot(xn_ref[...], w_ref[...])


def rms_matmul(x, g, w, *, exact, tm, tn):
    n, d = x.shape
    wd = w.shape[1]
    tm = min(tm, n)
    tn = min(tn, wd)
    return pl.pallas_call(
        functools.partial(_rms_mm_kernel, exact=exact),
        grid=(n // tm, wd // tn),
        in_specs=[
            pl.BlockSpec((tm, d), lambda i, j: (i, 0)),
            pl.BlockSpec((1, d), lambda i, j: (0, 0)),
            pl.BlockSpec((d, tn), lambda i, j: (0, j)),
        ],
        out_specs=pl.BlockSpec((tm, tn), lambda i, j: (i, j)),
        out_shape=jax.ShapeDtypeStruct((n, wd), F32),
        scratch_shapes=[pltpu.VMEM((tm, d), F32 if exact else BF16)],
        compiler_params=_cparams(("parallel", "arbitrary")),
        name="rms_matmul",
    )(x, g.reshape(1, d), w)


def _seg_matrix():
    r = lax.broadcasted_iota(jnp.int32, (LANES, LANES), 0) // HD
    c = lax.broadcasted_iota(jnp.int32, (LANES, LANES), 1) // HD
    return jnp.where(r == c, 1.0, 0.0).astype(BF16)


def _subhead_norm(x, g, seg):
    ss = _dot2_exact_rhs(x * x, seg)
    return x * lax.rsqrt(ss * (1.0 / HD) + EPS) * g


def _qk_norm_kernel(q_ref, k_ref, v_ref, g_ref, qb_ref, kn_ref, kb_ref, vb_ref, vf_ref):
    seg = _seg_matrix()
    gq = g_ref[0:1, :]
    gk = g_ref[1:2, :]
    for h in range(N_AH):
        sl = slice(h * LANES, (h + 1) * LANES)
        qn = _subhead_norm(q_ref[:, sl], gq, seg)
        kn = _subhead_norm(k_ref[:, sl], gk, seg)
        qb_ref[:, sl] = (qn * (HD ** -0.5)).astype(qb_ref.dtype)
        kn_ref[:, sl] = kn
        kb_ref[:, sl] = kn.astype(BF16)
    v = v_ref[...]
    vb_ref[...] = v.astype(BF16)
    vf_ref[...] = v


def qk_norm(y, g2, *, tm, q_dtype=BF16):
    n = y.shape[0]
    tm = min(tm, n)
    w = ATTN_W
    col = lambda c: pl.BlockSpec((tm, w), lambda i: (i, c))
    out = pl.BlockSpec((tm, w), lambda i: (i, 0))
    return pl.pallas_call(
        _qk_norm_kernel,
        grid=(n // tm,),
        in_specs=[col(COL_Q), col(COL_K), col(COL_V), pl.BlockSpec((2, LANES), lambda i: (0, 0))],
        out_specs=[out, out, out, out, out],
        out_shape=[
            jax.ShapeDtypeStruct((n, w), q_dtype),
            jax.ShapeDtypeStruct((n, w), F32),
            jax.ShapeDtypeStruct((n, w), BF16),
            jax.ShapeDtypeStruct((n, w), BF16),
            jax.ShapeDtypeStruct((n, w), F32),
        ],
        compiler_params=_cparams(("parallel",)),
        name="qk_norm",
    )(y, y, y, g2)


def _diff_lambda(lv, lam_init):
    t1 = jnp.sum(lv[0:1, :] * lv[1:2, :], axis=-1, keepdims=True)
    t2 = jnp.sum(lv[2:3, :] * lv[3:4, :], axis=-1, keepdims=True)
    return jnp.exp(t1) - jnp.exp(t2) + lam_init


def _attn_kernel(slope_ref, lv_ref, sg_ref, q_ref, k_ref, v_ref, o_ref, m_ref, acc_ref, *, tq, lam_init):
    h = pl.program_id(1)
    qi = pl.program_id(2)
    slope = slope_ref[h]
    q = q_ref[...]
    first = lax.broadcasted_iota(jnp.int32, (tq, LANES), 1) < HD
    col = lax.broadcasted_iota(jnp.int32, (1, tq), 1)
    ones = jnp.ones((tq, LANES), BF16)
    m_ref[...] = jnp.full(m_ref.shape, NEG, F32)
    acc_ref[...] = jnp.zeros_like(acc_ref)

    def update(c, s, v1):
        m_old = m_ref[c]
        m_new = jnp.maximum(m_old, jnp.max(s, axis=-1, keepdims=True))
        p = jnp.exp(s - m_new).astype(BF16)
        acc_ref[c] = acc_ref[c] * jnp.exp(m_old - m_new) + _dot(p, v1)
        m_ref[c] = m_new

    def block(j, masked):
        start = pl.multiple_of(j * tq, tq)
        kk = k_ref[pl.ds(start, tq), :]
        v1 = jnp.concatenate([v_ref[pl.ds(start, tq), :], ones], axis=-1)
        zero = jnp.zeros_like(kk)
        bias = slope * ((j - qi) * tq + col).astype(F32)
        for c in range(2):
            kc = jnp.where(first, kk, zero) if c == 0 else jnp.where(first, zero, kk)
            s = _dot_nt(q, kc) + bias
            if masked:
                row2 = lax.broadcasted_iota(jnp.int32, (tq, tq), 0)
                col2 = lax.broadcasted_iota(jnp.int32, (tq, tq), 1)
                s = jnp.where(col2 <= row2, s, NEG)
            update(c, s, v1)

    def body(j, carry):
        block(j, False)
        return carry

    lax.fori_loop(0, qi, body, 0)
    block(qi, True)

    lam = _diff_lambda(lv_ref[...], lam_init)
    r0 = acc_ref[0]
    r1 = acc_ref[1]
    o = r0[:, :LANES] / r0[:, LANES:] - lam * (r1[:, :LANES] / r1[:, LANES:])
    o_ref[...] = _rms(o, sg_ref[...]) * (1.0 - lam_init)


def attn_prompt(qb, kb, vb, lv, sg, slopes, *, b, s, lam_init, tq):
    n = b * s
    tq = min(tq, s)
    nq = s // tq
    return pl.pallas_call(
        functools.partial(_attn_kernel, tq=tq, lam_init=lam_init),
        grid=(b, N_AH, nq),
        in_specs=[
            pl.BlockSpec(memory_space=pltpu.SMEM),
            pl.BlockSpec((4, HD), lambda bi, h, i: (0, 0)),
            pl.BlockSpec((1, VD), lambda bi, h, i: (0, 0)),
            pl.BlockSpec((tq, LANES), lambda bi, h, i: (bi * nq + i, h)),
            pl.BlockSpec((s, LANES), lambda bi, h, i: (bi, h)),
            pl.BlockSpec((s, LANES), lambda bi, h, i: (bi, h)),
        ],
        out_specs=pl.BlockSpec((tq, LANES), lambda bi, h, i: (bi * nq + i, h)),
        out_shape=jax.ShapeDtypeStruct((n, ATTN_W), F32),
        scratch_shapes=[pltpu.VMEM((2, tq, 1), F32), pltpu.VMEM((2, tq, 2 * LANES), F32)],
        compiler_params=_cparams(("parallel", "parallel", "arbitrary")),
        name="attn_prompt",
    )(slopes, lv, sg.reshape(1, VD), qb, kb, vb)


def _scan_rows(a, u):
    tt = a.shape[0]
    row = lax.broadcasted_iota(jnp.int32, a.shape, 0)
    s = 1
    while s < tt:
        keep = row >= s
        a_prev = jnp.where(keep, pltpu.roll(a, s, 0), 1.0)
        u_prev = jnp.where(keep, pltpu.roll(u, s, 0), 0.0)
        u = a * u_prev + u
        a = a * a_prev
        s *= 2
    return a, u


def _lru_gates(xc, wa_ref, ba, wx_ref, bx, exact):
    rs, is_ = [], []
    for nb in range(N_LB):
        xs = xc[:, nb * LB:(nb + 1) * LB]
        if exact:
            rs.append(_dot3(xs, wa_ref[nb]))
            is_.append(_dot3(xs, wx_ref[nb]))
        else:
            xb = xs.astype(BF16)
            rs.append(_dot(xb, wa_ref[nb]))
            is_.append(_dot(xb, wx_ref[nb]))
    r = _sigmoid(jnp.concatenate(rs, axis=-1) + ba)
    i = _sigmoid(jnp.concatenate(is_, axis=-1) + bx)
    return r, i


def _lru_au(xc, r, i, lam):
    log_a = -LRU_C * r * _softplus(-lam)
    a = jnp.exp(log_a)
    u = jnp.sqrt(-_expm1(2.0 * log_a)) * (i * xc)
    return a, u


def _lru_kernel(lx_ref, lg_ref, cw_ref, cb_ref, wa_ref, ba_ref, wx_ref, bx_ref, lam_ref,
                y_ref, conv_ref, hl_ref, xbuf, hc, *, tt):
    t = pl.program_id(1)
    pad = SUBLANES

    @pl.when(t == 0)
    def _():
        xbuf[0:pad, :] = jnp.zeros((pad, LRU_W), F32)
        hc[...] = jnp.zeros_like(hc)

    x = lx_ref[...]
    xbuf[pad:pad + tt, :] = x
    cw = cw_ref[...]
    xc = cb_ref[...] + cw[3:4, :] * x
    for j in range(CONV_W - 1):
        back = CONV_W - 1 - j
        xc = xc + cw[j:j + 1, :] * xbuf[pad - back:pad - back + tt, :]
    xbuf[0:pad, :] = x[tt - pad:tt, :]

    r, i = _lru_gates(xc, wa_ref, ba_ref[...], wx_ref, bx_ref[...], False)
    a, u = _lru_au(xc, r, i, lam_ref[...])
    ca, cu = _scan_rows(a, u)
    hseq = ca * hc[0:1, :] + cu
    hc[0:1, :] = hseq[tt - 1:tt, :]
    y_ref[...] = (hseq * _gelu(lg_ref[...])).astype(y_ref.dtype)

    @pl.when(t == pl.num_programs(1) - 1)
    def _():
        conv_ref[...] = x[tt - (CONV_W - 1):tt, :]
        hl_ref[...] = hseq[tt - 1:tt, :]


def lru_prompt(y, cw, cb, wa, ba, wx, bx, lam, *, b, s, tt):
    n = b * s
    tt = min(tt, s)
    nt = s // tt
    vec = lambda: pl.BlockSpec((1, LRU_W), lambda bi, t: (0, 0))
    mat = lambda: pl.BlockSpec((N_LB, LB, LB), lambda bi, t: (0, 0, 0))
    return pl.pallas_call(
        functools.partial(_lru_kernel, tt=tt),
        grid=(b, nt),
        in_specs=[
            pl.BlockSpec((tt, LRU_W), lambda bi, t: (bi * nt + t, COL_LX)),
            pl.BlockSpec((tt, LRU_W), lambda bi, t: (bi * nt + t, COL_LG)),
            pl.BlockSpec((CONV_W, LRU_W), lambda bi, t: (0, 0)),
            vec(), mat(), vec(), mat(), vec(), vec(),
        ],
        out_specs=[
            pl.BlockSpec((tt, LRU_W), lambda bi, t: (bi * nt + t, 0)),
            pl.BlockSpec((None, CONV_W - 1, LRU_W), lambda bi, t: (bi, 0, 0)),
            pl.BlockSpec((None, 1, LRU_W), lambda bi, t: (bi, 0, 0)),
        ],
        out_shape=[
            jax.ShapeDtypeStruct((n, LRU_W), BF16),
            jax.ShapeDtypeStruct((b, CONV_W - 1, LRU_W), F32),
            jax.ShapeDtypeStruct((b, 1, LRU_W), F32),
        ],
        scratch_shapes=[pltpu.VMEM((tt + SUBLANES, LRU_W), F32), pltpu.VMEM((SUBLANES, LRU_W), F32)],
        compiler_params=_cparams(("parallel", "arbitrary")),
        name="lru_prompt",
    )(y, y, cw, cb.reshape(1, -1), wa.astype(BF16), ba.reshape(1, -1), wx.astype(BF16),
      bx.reshape(1, -1), lam.reshape(1, -1))


H_CHUNK = 64
H_SUB = SUBLANES


def _hgrn_lower_bound(rows, layer):
    mx = rows[0]
    for r in rows[1:]:
        mx = jnp.maximum(mx, r)
    es = [jnp.exp(r - mx) for r in rows]
    tot = es[0]
    for e in es[1:]:
        tot = tot + e
    lb = jnp.zeros_like(mx)
    for d in range(1, layer + 1):
        lb = lb + es[d] / tot
    return lb


def _hgrn_log_f(z, lb):
    log_sig = jnp.minimum(z, 0.0) - jnp.log1p(jnp.exp(-jnp.abs(z)))
    a = jnp.log(lb)
    bb = jnp.log1p(-lb) + log_sig
    return jnp.maximum(a, bb) + jnp.log1p(jnp.exp(-jnp.abs(a - bb)))


def _cumsum_rows(x):
    n = x.shape[0]
    row = lax.broadcasted_iota(jnp.int32, x.shape, 0)
    s = 1
    while s < n:
        x = x + jnp.where(row >= s, pltpu.roll(x, s, 0), 0.0)
        s *= 2
    return x


def _hgrn_chunk(qq, kk, logf, vv, st):
    c = qq.shape[0]
    nsub = c // H_SUB
    bcum = _cumsum_rows(logf)
    row_c = lax.broadcasted_iota(jnp.int32, (c, LANES), 0)
    row_s = lax.broadcasted_iota(jnp.int32, (H_SUB, LANES), 0)
    vb = vv.astype(BF16)

    o = _dot_nt((qq * jnp.exp(bcum)).astype(BF16), st.astype(BF16))

    att_rows = []
    diag_rows = []
    for i in range(nsub):
        r0 = i * H_SUB
        q_r = qq[r0:r0 + H_SUB, :]
        b_r = bcum[r0:r0 + H_SUB, :]
        if i == 0:
            att_rows.append(jnp.zeros((H_SUB, c), F32))
        else:
            beta = bcum[r0 - 1:r0, :]
            qt = q_r * jnp.exp(b_r - beta)
            kt = kk * jnp.exp(jnp.where(row_c < r0, beta - bcum, NEG))
            att_rows.append(_dot_nt(qt.astype(BF16), kt.astype(BF16)))
        k_r = kk[r0:r0 + H_SUB, :]
        v_r = vv[r0:r0 + H_SUB, :]
        od = jnp.zeros((H_SUB, LANES), F32)
        for s in range(H_SUB):
            e = jnp.exp(jnp.where(row_s >= s, b_r - b_r[s:s + 1, :], NEG))
            w = jnp.sum(q_r * k_r[s:s + 1, :] * e, axis=-1, keepdims=True)
            od = od + w * v_r[s:s + 1, :]
        diag_rows.append(od)
    att = jnp.concatenate(att_rows, axis=0)
    o = o + _dot(att.astype(BF16), vb) + jnp.concatenate(diag_rows, axis=0)

    b_last = bcum[c - 1:c, :]
    kd = (kk * jnp.exp(b_last - bcum)).astype(BF16)
    st = st * jnp.exp(b_last) + lax.dot_general(vb, kd, (((0,), (0,)), ((), ())),
                                                 preferred_element_type=F32)
    return o, st


def _hgrn_kernel(hq_ref, hf_ref, hi_ref, hg_ref, lb_ref, g_ref, y_ref, s_ref, st_ref, *, layer, nchunk):
    lb = _hgrn_lower_bound([lb_ref[d:d + 1, :] for d in range(DEPTH)], layer)
    g = g_ref[...]
    st_ref[...] = jnp.zeros_like(st_ref)

    def body(ci, carry):
        r0 = pl.multiple_of(ci * H_CHUNK, H_CHUNK)
        rows = pl.ds(r0, H_CHUNK)
        z = hf_ref[rows, :]
        logf = _hgrn_log_f(z, lb)
        kk = (1.0 - lb) * _sigmoid(-z)
        qq = _silu(hq_ref[rows, :])
        o, st = _hgrn_chunk(qq, kk, logf, hi_ref[rows, :], st_ref[...])
        st_ref[...] = st
        y_ref[rows, :] = (_rms(o, g) * _silu(hg_ref[rows, :])).astype(y_ref.dtype)
        return carry

    lax.fori_loop(0, nchunk, body, 0, unroll=4)
    s_ref[...] = st_ref[...]


def hgrn_prompt(y, lb_all, g, *, b, s, layer):
    n = b * s
    nchunk = s // H_CHUNK
    col = lambda c: pl.BlockSpec((s, LANES), lambda bi, h: (bi, c * N_HH + h))
    return pl.pallas_call(
        functools.partial(_hgrn_kernel, layer=layer, nchunk=nchunk),
        grid=(b, N_HH),
        in_specs=[
            col(COL_HQ), col(COL_HF), col(COL_HI), col(COL_HG),
            pl.BlockSpec((DEPTH, LANES), lambda bi, h: (0, h)),
            pl.BlockSpec((1, HV), lambda bi, h: (0, 0)),
        ],
        out_specs=[
            pl.BlockSpec((s, LANES), lambda bi, h: (bi, h)),
            pl.BlockSpec((None, None, HV, HK), lambda bi, h: (bi, h, 0, 0)),
        ],
        out_shape=[
            jax.ShapeDtypeStruct((n, HGRN_W), BF16),
            jax.ShapeDtypeStruct((b, N_HH, HV, HK), F32),
        ],
        scratch_shapes=[pltpu.VMEM((HV, HK), F32)],
        compiler_params=_cparams(("parallel", "parallel")),
        name="hgrn_prompt",
    )(y, y, y, y, lb_all, g.reshape(1, HV))


def _merge_kernel(x_ref, ya_ref, yl_ref, yh_ref, ga_ref, gl_ref, gh_ref,
                  wa_ref, wl_ref, wh_ref, wo_ref, o_ref, *, exact):
    if exact:
        mm = _dot3
        cast = lambda a: a.astype(F32)
    else:
        mm = _dot
        cast = lambda a: a.astype(BF16)
    m = _sigmoid(ga_ref[...]) * mm(cast(ya_ref[...]), wa_ref[...])
    m = m + _sigmoid(gl_ref[...]) * mm(cast(yl_ref[...]), wl_ref[...])
    m = m + _sigmoid(gh_ref[...]) * mm(cast(yh_ref[...]), wh_ref[...])
    o_ref[...] = x_ref[...] + mm(cast(m), wo_ref[...])


def merge_out(x, y, ya, yl, yh, wa, wl, wh, wo, *, exact, tm):
    n = x.shape[0]
    tm = min(tm, n)
    d = D_MODEL
    row = lambda: pl.BlockSpec((tm, d), lambda i: (i, 0))
    gate = lambda c: pl.BlockSpec((tm, d), lambda i: (i, COL_GT + c))
    wspec = lambda: pl.BlockSpec((d, d), lambda i: (0, 0))
    wdt = F32 if exact else BF16
    return pl.pallas_call(
        functools.partial(_merge_kernel, exact=exact),
        grid=(n // tm,),
        in_specs=[row(), row(), row(), row(), gate(0), gate(1), gate(2), wspec(), wspec(), wspec(), wspec()],
        out_specs=row(),
        out_shape=jax.ShapeDtypeStruct((n, d), F32),
        compiler_params=_cparams(("parallel",)),
        name="merge_out",
    )(x, ya, yl, yh, y, y, y, wa.astype(wdt), wl.astype(wdt), wh.astype(wdt), wo.astype(wdt))


def _topk_rows(x, k):
    out = []
    cur = x
    for _ in range(k):
        mx = jnp.max(cur, axis=0, keepdims=True)
        out.append(mx)
        cur = jnp.where(cur == mx, NEG, cur)
    return out


def _peer_select_kernel(q_ref, kh_ref, sel_ref):
    st = _dot3_nt(kh_ref[...], q_ref[...])
    s1 = st[:N_KEYS, :]
    s2 = st[N_KEYS:, :]
    v1 = _topk_rows(s1, P_TOPK)
    v2 = _topk_rows(s2, P_TOPK)
    v2m = jnp.concatenate(v2, axis=0)
    cands = [v1[a] + v2m for a in range(P_TOPK)]

    cur = cands
    tau = None
    for _ in range(P_TOPK):
        mx = cur[0]
        for c in cur[1:]:
            mx = jnp.maximum(mx, c)
        tau = jnp.max(mx, axis=0, keepdims=True)
        cur = [jnp.where(c == tau, NEG, c) for c in cur]

    top = v1[0] + v2[0]
    z = jnp.zeros_like(tau)
    theta_dense = jnp.full(s1.shape, -NEG, F32)
    for a in range(P_TOPK):
        sel = cands[a] >= tau
        z = z + jnp.sum(jnp.where(sel, jnp.exp(cands[a] - top), 0.0), axis=0, keepdims=True)
        theta_a = jnp.min(jnp.where(sel, v2m, -NEG), axis=0, keepdims=True)
        theta_dense = jnp.where(s1 == v1[a], theta_a, theta_dense)
    sel_ref[0] = s2
    sel_ref[1] = jnp.exp(s2 - v2[0])
    sel_ref[2] = theta_dense
    sel_ref[3] = jnp.exp(s1 - v1[0]) / z


def peer_select(q, kbd, *, tn):
    n = q.shape[0]
    tn = min(tn, n)
    return pl.pallas_call(
        _peer_select_kernel,
        grid=(n // tn, P_HEADS),
        in_specs=[
            pl.BlockSpec((tn, D_KEY), lambda i, h: (i, h)),
            pl.BlockSpec((None, 2 * N_KEYS, D_KEY), lambda i, h: (h, 0, 0)),
        ],
        out_specs=pl.BlockSpec((None, 4, N_KEYS, tn), lambda i, h: (h, 0, 0, i)),
        out_shape=jax.ShapeDtypeStruct((P_HEADS, 4, N_KEYS, n), F32),
        compiler_params=_cparams(("parallel", "parallel")),
        name="peer_select",
    )(q, kbd)


PEER_II_GROUP = 4


def _peer_dense_kernel(xnt_ref, x_ref, sel_ref, u_ref, vt_ref, o_ref, acc_ref, w_ref, sp_ref, bc_ref, *, ec):
    e = pl.program_id(1)
    tn = w_ref.shape[1]
    n_i1 = ec // N_KEYS
    nt = tn // LANES
    tile = lambda t: slice((t % nt) * LANES, (t % nt + 1) * LANES)
    rot_sp = lambda h, k: h + 2 * k
    rot_bc = lambda ii, h, k: h + 2 * k + ii + 1

    @pl.when(e == 0)
    def _():
        acc_ref[...] = jnp.zeros_like(acc_ref)
        for h in range(P_HEADS):
            for k in range(2):
                for t in range(nt):
                    sp_ref[h, k, :, tile(t + rot_sp(h, k))] = sel_ref[h, k, :, tile(t)]

    for ii in range(n_i1):
        i1 = e * n_i1 + ii
        for h in range(P_HEADS):
            for k in range(2):
                row = sel_ref[h, 2 + k, pl.ds(i1, 1), :]
                for t in range(nt):
                    bc_ref[ii, h, k, :, tile(t + rot_bc(ii, h, k))] = jnp.broadcast_to(row[:, tile(t)],
                                                                                      (SUBLANES, LANES))

    def rows_body(r, carry):
        r0 = pl.multiple_of(r * SUBLANES, SUBLANES)
        rows = pl.ds(r0, SUBLANES)
        for g0 in range(0, n_i1, PEER_II_GROUP):
            group = range(g0, min(g0 + PEER_II_GROUP, n_i1))
            w = [{ii: None for ii in group} for _ in range(nt)]
            for h in range(P_HEADS):
                s2 = [sp_ref[h, 0, rows, tile(t + rot_sp(h, 0))] for t in range(nt)]
                p2 = [sp_ref[h, 1, rows, tile(t + rot_sp(h, 1))] for t in range(nt)]
                for ii in group:
                    for t in range(nt):
                        theta = bc_ref[ii, h, 0, :, tile(t + rot_bc(ii, h, 0))]
                        c1 = bc_ref[ii, h, 1, :, tile(t + rot_bc(ii, h, 1))]
                        term = jnp.where(s2[t] >= theta, p2[t], 0.0) * c1
                        w[t][ii] = term if w[t][ii] is None else w[t][ii] + term
            for t in range(nt):
                for ii in group:
                    w_ref[pl.ds(ii * N_KEYS + r0, SUBLANES), tile(t)] = w[t][ii]
        return carry

    lax.fori_loop(0, N_KEYS // SUBLANES, rows_body, 0)
    a_t = _dot(u_ref[...], xnt_ref[...])
    h_t = (w_ref[...] * _gelu(a_t)).astype(BF16)
    acc_ref[...] += _dot(vt_ref[...], h_t)

    @pl.when(e == pl.num_programs(1) - 1)
    def _():
        o_ref[...] = x_ref[...] + acc_ref[...].T


def peer_dense(xnt, x, sel, u, vt, *, tn, ec):
    n = x.shape[0]
    tn = min(tn, n)
    d = D_MODEL
    return pl.pallas_call(
        functools.partial(_peer_dense_kernel, ec=ec),
        grid=(n // tn, N_EXP // ec),
        in_specs=[
            pl.BlockSpec((d, tn), lambda i, e: (0, i)),
            pl.BlockSpec((tn, d), lambda i, e: (i, 0)),
            pl.BlockSpec((P_HEADS, 4, N_KEYS, tn), lambda i, e: (0, 0, 0, i)),
            pl.BlockSpec((ec, d), lambda i, e: (e, 0)),
            pl.BlockSpec((d, ec), lambda i, e: (0, e)),
        ],
        out_specs=pl.BlockSpec((tn, d), lambda i, e: (i, 0)),
        out_shape=jax.ShapeDtypeStruct((n, d), F32),
        scratch_shapes=[pltpu.VMEM((d, tn), F32), pltpu.VMEM((ec, tn), F32),
                        pltpu.VMEM((P_HEADS, 2, N_KEYS, tn), F32),
                        pltpu.VMEM((ec // N_KEYS, P_HEADS, 2, SUBLANES, tn), F32)],
        compiler_params=_cparams(("parallel", "arbitrary")),
        name="peer_dense",
    )(xnt, x, sel, u, vt)


def _rms_cast_t_kernel(x_ref, g_ref, o_ref):
    o_ref[...] = _rms(x_ref[...], g_ref[...]).T.astype(o_ref.dtype)


def rms_cast_t(x, g, *, tm):
    n, d = x.shape
    tm = min(tm, n)
    return pl.pallas_call(
        _rms_cast_t_kernel,
        grid=(n // tm,),
        in_specs=[pl.BlockSpec((tm, d), lambda i: (i, 0)), pl.BlockSpec((1, d), lambda i: (0, 0))],
        out_specs=pl.BlockSpec((d, tm), lambda i: (0, i)),
        out_shape=jax.ShapeDtypeStruct((d, n), BF16),
        compiler_params=_cparams(("parallel",)),
        name="rms_cast_t",
    )(x, g.reshape(1, d))


def _peer_key_blocks(keys):
    z = jnp.zeros_like(keys[:, 0])
    top = jnp.concatenate([keys[:, 0], z], axis=-1)
    bot = jnp.concatenate([z, keys[:, 1]], axis=-1)
    return jnp.concatenate([top, bot], axis=1)


def peer_ffn(x, g, wq, kbd, u_b, vt_b, *, tn_sel, tn, ec):
    n = x.shape[0]
    pad = (-n) % LANES
    xp = jnp.pad(x, ((0, pad), (0, 0))) if pad else x
    q = rms_matmul(xp, g, wq, exact=True, tm=512, tn=1024)
    sel = peer_select(q, kbd, tn=tn_sel)
    xnt = rms_cast_t(xp, g, tm=512)
    out = peer_dense(xnt, xp, sel, u_b, vt_b, tn=tn, ec=ec)
    return out[:n] if pad else out


def _lam_init(layer):
    return 0.8 - 0.6 * math.exp(-0.3 * layer)


def _alibi_slopes():
    return jnp.exp2(-8.0 * jnp.arange(1, N_AH + 1, dtype=F32) / N_AH)


def layer_prompt(x, l, p, w_in_b, kbd, u_b, vt_b, *, b, s):
    n = b * s
    y = rms_matmul(x, p['norm1_g'][l], w_in_b, exact=False, tm=1024, tn=1024)
    g2 = jnp.tile(p['qk_norm_g'][l], (1, 2))
    qb, kn, kb, vb, vf = qk_norm(y, g2, tm=512)
    ya = attn_prompt(qb, kb, vb, p['diff_lambda'][l], p['subln_g'][l], _alibi_slopes(),
                     b=b, s=s, lam_init=_lam_init(l), tq=1024)
    yl, conv_new, h_last = lru_prompt(y, p['conv_w'][l], p['conv_b'][l], p['rg_wa'][l], p['rg_ba'][l],
                                      p['rg_wx'][l], p['rg_bx'][l], p['rg_lambda'][l], b=b, s=s, tt=256)
    yh, st = hgrn_prompt(y, p['hgrn_lb'], p['hgrn_norm_g'][l], b=b, s=s, layer=l)
    x = merge_out(x, y, ya, yl, yh, p['w_br_attn'][l], p['w_br_lru'][l], p['w_br_hgrn'][l],
                  p['w_out'][l], exact=False, tm=512)
    x = peer_ffn(x, p['norm2_g'][l], p['peer_wq'][l], kbd, u_b, vt_b, tn_sel=256, tn=512, ec=1024)
    k_out = kn.reshape(b, s, N_AH, 2 * HD)
    v_out = vf.reshape(b, s, N_AH, VD)
    s_out = jnp.swapaxes(st, -1, -2)
    return x, (k_out, v_out, conv_new, h_last.reshape(b, LRU_W), s_out)


DEC_PPS = 4


def _swap_halves(a):
    return pltpu.roll(a, HD, a.ndim - 1)


def _attn_decode_kernel(pt_ref, q_ref, kn_ref, vn_ref, slope_ref, lv_ref, sg_ref, *refs,
                        pps, past, lam_init):
    k_refs = refs[:pps]
    v_refs = refs[pps:2 * pps]
    o_ref, m_ref, l_ref, acc_ref = refs[2 * pps:]
    step = pl.program_id(1)

    @pl.when(step == 0)
    def _():
        m_ref[...] = jnp.full(m_ref.shape, NEG, F32)
        l_ref[...] = jnp.zeros_like(l_ref)
        acc_ref[...] = jnp.zeros_like(acc_ref)

    q = q_ref[...]
    seg = _seg_matrix()
    slope = slope_ref[...]
    tbias = slope[None] * lax.broadcasted_iota(jnp.int32, (PAGE_SIZE, N_AH, LANES), 0).astype(F32)

    def accumulate(sc, vals, off, reduce):
        m_old = m_ref[...]
        m_new = jnp.maximum(m_old, (jnp.max(sc, axis=0) if reduce else sc) + off)
        alpha = jnp.exp(m_old - m_new)
        pw = jnp.exp(sc - (m_new - off))
        pw_x = _swap_halves(pw)
        red = (lambda a: jnp.sum(a, axis=0)) if reduce else (lambda a: a)
        l_ref[...] = l_ref[...] * alpha + red(pw)
        acc_ref[0] = acc_ref[0] * alpha + red(pw * vals)
        acc_ref[1] = acc_ref[1] * _swap_halves(alpha) + red(pw_x * vals)
        m_ref[...] = m_new

    for i in range(pps):
        kpage = k_refs[i][...]
        prod = (kpage * q[None]).reshape(PAGE_SIZE * N_AH, LANES).astype(BF16)
        s = _dot(prod, seg).reshape(PAGE_SIZE, N_AH, LANES)
        off = slope * ((step * pps + i) * PAGE_SIZE - past).astype(F32)
        accumulate(s + tbias, v_refs[i][...], off, True)

    @pl.when(step == pl.num_programs(1) - 1)
    def _():
        sn = _dot2_exact_rhs(kn_ref[...] * q, seg)
        accumulate(sn, vn_ref[...], 0.0, False)
        first = lax.broadcasted_iota(jnp.int32, (N_AH, LANES), 1) < HD
        l = l_ref[...]
        l_x = _swap_halves(l)
        a_s = acc_ref[0]
        a_x = acc_ref[1]
        o0 = jnp.where(first, a_s, a_x) / jnp.where(first, l, l_x)
        o1 = jnp.where(first, a_x, a_s) / jnp.where(first, l_x, l)
        lam = _diff_lambda(lv_ref[...], lam_init)
        o = o0 - lam * o1
        o_ref[...] = _rms(o, sg_ref[...]) * (1.0 - lam_init)


def attn_decode(page_table, q, kn, vn, cache_k, cache_v, lv, sg, slopes, *, layer, lam_init):
    bd, n_pages = page_table.shape
    pps = DEC_PPS
    past = n_pages * PAGE_SIZE
    tok = lambda: pl.BlockSpec((None, N_AH, LANES), lambda b, st, pt: (b, 0, 0))
    page = lambda i: pl.BlockSpec((None, None, PAGE_SIZE, N_AH, LANES),
                                  lambda b, st, pt: (layer, pt[b, st * pps + i], 0, 0, 0))
    grid_spec = pltpu.PrefetchScalarGridSpec(
        num_scalar_prefetch=1,
        grid=(bd, n_pages // pps),
        in_specs=[
            tok(), tok(), tok(),
            pl.BlockSpec((N_AH, LANES), lambda b, st, pt: (0, 0)),
            pl.BlockSpec((4, HD), lambda b, st, pt: (0, 0)),
            pl.BlockSpec((1, VD), lambda b, st, pt: (0, 0)),
        ] + [page(i) for i in range(pps)] + [page(i) for i in range(pps)],
        out_specs=pl.BlockSpec((None, N_AH, LANES), lambda b, st, pt: (b, 0, 0)),
        scratch_shapes=[pltpu.VMEM((N_AH, LANES), F32), pltpu.VMEM((N_AH, LANES), F32),
                        pltpu.VMEM((2, N_AH, LANES), F32)],
    )
    slope_tile = jnp.broadcast_to(slopes[:, None], (N_AH, LANES))
    return pl.pallas_call(
        functools.partial(_attn_decode_kernel, pps=pps, past=past, lam_init=lam_init),
        grid_spec=grid_spec,
        out_shape=jax.ShapeDtypeStruct((bd, N_AH, LANES), F32),
        compiler_params=_cparams(("parallel", "arbitrary")),
        name="attn_decode",
    )(page_table, q, kn, vn, slope_tile, lv, sg.reshape(1, VD), *([cache_k] * pps), *([cache_v] * pps))


def _lru_step_kernel(lx_ref, lg_ref, buf_ref, h0_ref, cw_ref, cb_ref, wa_ref, ba_ref, wx_ref, bx_ref,
                     lam_ref, yl_ref, conv_ref, h_ref):
    x = lx_ref[...]
    cw = cw_ref[...]
    xc = cb_ref[...] + cw[CONV_W - 1:CONV_W, :] * x
    for j in range(CONV_W - 1):
        xc = xc + cw[j:j + 1, :] * buf_ref[j]
    r, i = _lru_gates(xc, wa_ref, ba_ref[...], wx_ref, bx_ref[...], True)
    a, u = _lru_au(xc, r, i, lam_ref[...])
    h = a * h0_ref[...] + u
    yl_ref[...] = h * _gelu(lg_ref[...])
    for j in range(CONV_W - 2):
        conv_ref[j] = buf_ref[j + 1]
    conv_ref[CONV_W - 2] = x
    h_ref[...] = h


def lru_step(y, buf_t, h0, cw, cb, wa, ba, wx, bx, lam):
    bd = y.shape[0]
    full = lambda shape: pl.BlockSpec(shape, lambda i: (0,) * len(shape))
    return pl.pallas_call(
        _lru_step_kernel,
        grid=(1,),
        in_specs=[
            pl.BlockSpec((bd, LRU_W), lambda i: (0, COL_LX)),
            pl.BlockSpec((bd, LRU_W), lambda i: (0, COL_LG)),
            full((CONV_W - 1, bd, LRU_W)), full((bd, LRU_W)), full((CONV_W, LRU_W)), full((1, LRU_W)),
            full((N_LB, LB, LB)), full((1, LRU_W)), full((N_LB, LB, LB)), full((1, LRU_W)), full((1, LRU_W)),
        ],
        out_specs=[full((bd, LRU_W)), full((CONV_W - 1, bd, LRU_W)), full((bd, LRU_W))],
        out_shape=[
            jax.ShapeDtypeStruct((bd, LRU_W), F32),
            jax.ShapeDtypeStruct((CONV_W - 1, bd, LRU_W), F32),
            jax.ShapeDtypeStruct((bd, LRU_W), F32),
        ],
        compiler_params=_cparams(("arbitrary",)),
        name="lru_step",
    )(y, y, buf_t, h0, cw, cb.reshape(1, -1), wa, ba.reshape(1, -1), wx, bx.reshape(1, -1), lam.reshape(1, -1))


def _hgrn_step_kernel(hq_ref, hf_ref, hi_ref, hg_ref, lb_ref, g_ref, s0_ref, y_ref, s1_ref, *, layer):
    lb = _hgrn_lower_bound([lb_ref[d] for d in range(DEPTH)], layer)
    z = hf_ref[...]
    f = jnp.exp(_hgrn_log_f(z, lb))
    kk = (1.0 - lb) * _sigmoid(-z)
    qq = _silu(hq_ref[...])
    vv = hi_ref[...]
    pad = jnp.zeros((HK - N_HH, LANES), F32)
    cols = lambda a: jnp.concatenate([a, pad], axis=0).T
    q_t, f_t, k_t = cols(qq), cols(f), cols(kk)
    row = lax.broadcasted_iota(jnp.int32, (N_HH, LANES), 0)
    o = jnp.zeros((N_HH, LANES), F32)
    for h in range(N_HH):
        s1 = s0_ref[h] * f_t[:, h:h + 1] + k_t[:, h:h + 1] * vv[h:h + 1, :]
        s1_ref[h] = s1
        oh = jnp.sum(q_t[:, h:h + 1] * s1, axis=0, keepdims=True)
        o = jnp.where(row == h, oh, o)
    y_ref[...] = _rms(o, g_ref[...]) * _silu(hg_ref[...])


def hgrn_step(y3, lb3, g, state, *, layer):
    bd = y3.shape[0]
    col = lambda c: pl.BlockSpec((None, N_HH, LANES), lambda b: (b, c, 0))
    return pl.pallas_call(
        functools.partial(_hgrn_step_kernel, layer=layer),
        grid=(bd,),
        in_specs=[
            col(COL_HQ), col(COL_HF), col(COL_HI), col(COL_HG),
            pl.BlockSpec((DEPTH, N_HH, LANES), lambda b: (0, 0, 0)),
            pl.BlockSpec((1, HV), lambda b: (0, 0)),
            pl.BlockSpec((None, None, N_HH, HK, HV), lambda b: (layer, b, 0, 0, 0)),
        ],
        out_specs=[
            pl.BlockSpec((None, N_HH, LANES), lambda b: (b, 0, 0)),
            pl.BlockSpec((None, N_HH, HK, HV), lambda b: (b, 0, 0, 0)),
        ],
        out_shape=[
            jax.ShapeDtypeStruct((bd, N_HH, LANES), F32),
            jax.ShapeDtypeStruct((bd, N_HH, HK, HV), F32),
        ],
        compiler_params=_cparams(("parallel",)),
        name="hgrn_step",
    )(y3, y3, y3, y3, lb3, g.reshape(1, HV), state)


def layer_sample(x, l, p, cache_k, cache_v, page_table, state_conv, state_lru, state_hgrn, kbd, u_b, vt_b):
    bd = x.shape[0]
    y = rms_matmul(x, p['norm1_g'][l], p['w_in'][l], exact=True, tm=bd, tn=1024)
    g2 = jnp.tile(p['qk_norm_g'][l], (1, 2))
    qf, kn, _, _, v_new = qk_norm(y, g2, tm=bd, q_dtype=F32)
    tok = lambda a: a.reshape(bd, N_AH, LANES)
    ya = attn_decode(page_table, tok(qf), tok(kn), tok(v_new), cache_k, cache_v, p['diff_lambda'][l],
                     p['subln_g'][l], _alibi_slopes(), layer=l, lam_init=_lam_init(l))
    yl, conv_t, h_new = lru_step(y, jnp.swapaxes(state_conv[l], 0, 1), state_lru[l], p['conv_w'][l],
                                 p['conv_b'][l], p['rg_wa'][l], p['rg_ba'][l], p['rg_wx'][l], p['rg_bx'][l],
                                 p['rg_lambda'][l])
    yh, s1 = hgrn_step(y.reshape(bd, IN_W // LANES, LANES), p['hgrn_lb'].reshape(DEPTH, N_HH, LANES),
                       p['hgrn_norm_g'][l], state_hgrn, layer=l)
    x = merge_out(x, y, ya.reshape(bd, ATTN_W), yl, yh.reshape(bd, HGRN_W), p['w_br_attn'][l],
                  p['w_br_lru'][l], p['w_br_hgrn'][l], p['w_out'][l], exact=True, tm=bd)
    x = peer_ffn(x, p['norm2_g'][l], p['peer_wq'][l], kbd, u_b, vt_b, tn_sel=256, tn=512, ec=1024)
    return x, (kn.reshape(bd, 1, N_AH, 2 * HD), v_new.reshape(bd, 1, N_AH, VD), jnp.swapaxes(conv_t, 0, 1),
               h_new, s1)


def kernel(x_prompt, x_sample, cache_k, cache_v, state_conv, state_lru, state_hgrn, page_table,
           norm1_g, norm2_g, w_in, qk_norm_g, diff_lambda, subln_g, conv_w, conv_b,
           rg_wa, rg_ba, rg_wx, rg_bx, rg_lambda, hgrn_lb, hgrn_norm_g,
           w_br_attn, w_br_lru, w_br_hgrn, w_out, peer_wq, peer_keys, peer_u, peer_v):
    p = dict(norm1_g=norm1_g, norm2_g=norm2_g, w_in=w_in, qk_norm_g=qk_norm_g, diff_lambda=diff_lambda,
             subln_g=subln_g, conv_w=conv_w, conv_b=conv_b, rg_wa=rg_wa, rg_ba=rg_ba, rg_wx=rg_wx,
             rg_bx=rg_bx, rg_lambda=rg_lambda, hgrn_lb=hgrn_lb, hgrn_norm_g=hgrn_norm_g,
             w_br_attn=w_br_attn, w_br_lru=w_br_lru, w_br_hgrn=w_br_hgrn, w_out=w_out,
             peer_wq=peer_wq, peer_keys=peer_keys, peer_u=peer_u, peer_v=peer_v)
    b, s, _ = x_prompt.shape
    bd = x_sample.shape[0]
    xp = x_prompt.reshape(b * s, D_MODEL)
    xs = x_sample.reshape(bd, D_MODEL)
    st_p, st_s = [], []
    for l in range(DEPTH):
        w_in_b = w_in[l].astype(BF16)
        kbd = _peer_key_blocks(peer_keys[l])
        u_b = peer_u[l].astype(BF16)
        vt_b = peer_v[l].T.astype(BF16)
        xp, sp = layer_prompt(xp, l, p, w_in_b, kbd, u_b, vt_b, b=b, s=s)
        xs, ss = layer_sample(xs, l, p, cache_k, cache_v, page_table, state_conv, state_lru, state_hgrn,
                              kbd, u_b, vt_b)
        st_p.append(sp)
        st_s.append(ss)
    k_p, v_p, conv_p, lru_p, hgrn_p = [jnp.stack(a) for a in zip(*st_p)]
    k_s, v_s, conv_s, lru_s, hgrn_s = [jnp.stack(a) for a in zip(*st_s)]
    return (xp.reshape(b, s, D_MODEL), xs.reshape(bd, 1, D_MODEL), k_p, v_p, conv_p, lru_p, hgrn_p,
            k_s, v_s, conv_s, lru_s, hgrn_s)
```

```python
import functools
import math

import jax
import jax.numpy as jnp
from jax import lax
from jax.experimental import pallas as pl
from jax.experimental.pallas import tpu as pltpu

F32 = jnp.float32
BF16 = jnp.bfloat16

D_MODEL = 1024
DEPTH = 2
PAGE_SIZE = 128
N_AH = 8
HD = 64
VD = 2 * HD
ATTN_W = N_AH * VD
LRU_W = 1024
N_LB = 8
LB = LRU_W // N_LB
CONV_W = 4
LRU_C = 8.0
N_HH = 8
HK = 128
HV = 128
HGRN_W = N_HH * HV
P_HEADS = 8
N_KEYS = 128
N_EXP = N_KEYS * N_KEYS
D_KEY = 128
P_TOPK = 16
EPS = 1e-6
IN_W = 12 * D_MODEL

COL_Q, COL_K, COL_V, COL_LX, COL_LG, COL_HQ, COL_HF, COL_HI, COL_HG, COL_GT = 0, 1, 2, 3, 4, 5, 6, 7, 8, 9

LANES = 128
SUBLANES = 8
VMEM_LIMIT = 56 * 1024 * 1024

NEG = -1e30
SQRT_HALF = 0.7071067811865476


def _cparams(sem, flags=None):
    return pltpu.CompilerParams(dimension_semantics=sem, vmem_limit_bytes=VMEM_LIMIT, flags=flags)


def _split(a):
    hi = a.astype(BF16)
    lo = (a - hi.astype(F32)).astype(BF16)
    return hi, lo


def _dot(a, b):
    return jnp.dot(a, b, preferred_element_type=F32)


def _dot_nt(a, b):
    return lax.dot_general(a, b, (((1,), (1,)), ((), ())), preferred_element_type=F32)


def _dot3(a, b):
    ah, al = _split(a)
    bh, bl = _split(b)
    return _dot(ah, bh) + _dot(al, bh) + _dot(ah, bl)


def _dot3_nt(a, b):
    ah, al = _split(a)
    bh, bl = _split(b)
    return _dot_nt(ah, bh) + _dot_nt(al, bh) + _dot_nt(ah, bl)


def _dot2_exact_rhs(a, b_bf16):
    ah, al = _split(a)
    return _dot(ah, b_bf16) + _dot(al, b_bf16)


def _sigmoid(x):
    return 1.0 / (1.0 + jnp.exp(-x))


def _gelu(x):
    return 0.5 * x * (1.0 + lax.erf(x * SQRT_HALF))


def _silu(x):
    return x * _sigmoid(x)


def _softplus(x):
    return jnp.maximum(x, 0.0) + jnp.log1p(jnp.exp(-jnp.abs(x)))


def _expm1(x):
    u = jnp.exp(x)
    um1 = u - 1.0
    corrected = um1 * x / jnp.log(u)
    return jnp.where(um1 == 0.0, x, jnp.where(um1 == -1.0, -1.0, corrected))


def _rms(x, g):
    ms = jnp.mean(x * x, axis=-1, keepdims=True)
    return x * lax.rsqrt(ms + EPS) * g


def _rms_mm_kernel(x_ref, g_ref, w_ref, o_ref, xn_ref, *, exact):
    @pl.when(pl.program_id(1) == 0)
    def _():
        xn_ref[...] = _rms(x_ref[...], g_ref[...]).astype(xn_ref.dtype)

    if exact:
        o_ref[...] = _dot3(xn_ref[...], w_ref[...])
    else:
        o_ref[...] = _dot(xn_ref[...], w_ref[...])


def rms_matmul(x, g, w, *, exact, tm, tn):
    n, d = x.shape
    wd = w.shape[1]
    tm = min(tm, n)
    tn = min(tn, wd)
    return pl.pallas_call(
        functools.partial(_rms_mm_kernel, exact=exact),
        grid=(n // tm, wd // tn),
        in_specs=[
            pl.BlockSpec((tm, d), lambda i, j: (i, 0)),
            pl.BlockSpec((1, d), lambda i, j: (0, 0)),
            pl.BlockSpec((d, tn), lambda i, j: (0, j)),
        ],
        out_specs=pl.BlockSpec((tm, tn), lambda i, j: (i, j)),
        out_shape=jax.ShapeDtypeStruct((n, wd), F32),
        scratch_shapes=[pltpu.VMEM((tm, d), F32 if exact else BF16)],
        compiler_params=_cparams(("parallel", "arbitrary")),
        name="rms_matmul",
    )(x, g.reshape(1, d), w)


def _seg_matrix():
    r = lax.broadcasted_iota(jnp.int32, (LANES, LANES), 0) // HD
    c = lax.broadcasted_iota(jnp.int32, (LANES, LANES), 1) // HD
    return jnp.where(r == c, 1.0, 0.0).astype(BF16)


def _subhead_norm(x, g, seg):
    ss = _dot2_exact_rhs(x * x, seg)
    return x * lax.rsqrt(ss * (1.0 / HD) + EPS) * g


def _qk_norm_kernel(q_ref, k_ref, v_ref, g_ref, qb_ref, kn_ref, kb_ref, vb_ref, vf_ref):
    seg = _seg_matrix()
    gq = g_ref[0:1, :]
    gk = g_ref[1:2, :]
    for h in range(N_AH):
        sl = slice(h * LANES, (h + 1) * LANES)
        qn = _subhead_norm(q_ref[:, sl], gq, seg)
        kn = _subhead_norm(k_ref[:, sl], gk, seg)
        qb_ref[:, sl] = (qn * (HD ** -0.5)).astype(qb_ref.dtype)
        kn_ref[:, sl] = kn
        kb_ref[:, sl] = kn.astype(BF16)
    v = v_ref[...]
    vb_ref[...] = v.astype(BF16)
    vf_ref[...] = v


def qk_norm(y, g2, *, tm, q_dtype=BF16):
    n = y.shape[0]
    tm = min(tm, n)
    w = ATTN_W
    col = lambda c: pl.BlockSpec((tm, w), lambda i: (i, c))
    out = pl.BlockSpec((tm, w), lambda i: (i, 0))
    return pl.pallas_call(
        _qk_norm_kernel,
        grid=(n // tm,),
        in_specs=[col(COL_Q), col(COL_K), col(COL_V), pl.BlockSpec((2, LANES), lambda i: (0, 0))],
        out_specs=[out, out, out, out, out],
        out_shape=[
            jax.ShapeDtypeStruct((n, w), q_dtype),
            jax.ShapeDtypeStruct((n, w), F32),
            jax.ShapeDtypeStruct((n, w), BF16),
            jax.ShapeDtypeStruct((n, w), BF16),
            jax.ShapeDtypeStruct((n, w), F32),
        ],
        compiler_params=_cparams(("parallel",)),
        name="qk_norm",
    )(y, y, y, g2)


def _diff_lambda(lv, lam_init):
    t1 = jnp.sum(lv[0:1, :] * lv[1:2, :], axis=-1, keepdims=True)
    t2 = jnp.sum(lv[2:3, :] * lv[3:4, :], axis=-1, keepdims=True)
    return jnp.exp(t1) - jnp.exp(t2) + lam_init


def _attn_kernel(slope_ref, lv_ref, sg_ref, q_ref, k_ref, v_ref, o_ref, m_ref, acc_ref, *, tq, lam_init):
    h = pl.program_id(1)
    qi = pl.program_id(2)
    slope = slope_ref[h]
    q = q_ref[...]
    first = lax.broadcasted_iota(jnp.int32, (tq, LANES), 1) < HD
    col = lax.broadcasted_iota(jnp.int32, (1, tq), 1)
    ones = jnp.ones((tq, LANES), BF16)
    m_ref[...] = jnp.full(m_ref.shape, NEG, F32)
    acc_ref[...] = jnp.zeros_like(acc_ref)

    def update(c, s, v1):
        m_old = m_ref[c]
        m_new = jnp.maximum(m_old, jnp.max(s, axis=-1, keepdims=True))
        p = jnp.exp(s - m_new).astype(BF16)
        acc_ref[c] = acc_ref[c] * jnp.exp(m_old - m_new) + _dot(p, v1)
        m_ref[c] = m_new

    def block(j, masked):
        start = pl.multiple_of(j * tq, tq)
        kk = k_ref[pl.ds(start, tq), :]
        v1 = jnp.concatenate([v_ref[pl.ds(start, tq), :], ones], axis=-1)
        zero = jnp.zeros_like(kk)
        bias = slope * ((j - qi) * tq + col).astype(F32)
        for c in range(2):
            kc = jnp.where(first, kk, zero) if c == 0 else jnp.where(first, zero, kk)
            s = _dot_nt(q, kc) + bias
            if masked:
                row2 = lax.broadcasted_iota(jnp.int32, (tq, tq), 0)
                col2 = lax.broadcasted_iota(jnp.int32, (tq, tq), 1)
                s = jnp.where(col2 <= row2, s, NEG)
            update(c, s, v1)

    def body(j, carry):
        block(j, False)
        return carry

    lax.fori_loop(0, qi, body, 0)
    block(qi, True)

    lam = _diff_lambda(lv_ref[...], lam_init)
    r0 = acc_ref[0]
    r1 = acc_ref[1]
    o = r0[:, :LANES] / r0[:, LANES:] - lam * (r1[:, :LANES] / r1[:, LANES:])
    o_ref[...] = _rms(o, sg_ref[...]) * (1.0 - lam_init)


def attn_prompt(qb, kb, vb, lv, sg, slopes, *, b, s, lam_init, tq):
    n = b * s
    tq = min(tq, s)
    nq = s // tq
    return pl.pallas_call(
        functools.partial(_attn_kernel, tq=tq, lam_init=lam_init),
        grid=(b, N_AH, nq),
        in_specs=[
            pl.BlockSpec(memory_space=pltpu.SMEM),
            pl.BlockSpec((4, HD), lambda bi, h, i: (0, 0)),
            pl.BlockSpec((1, VD), lambda bi, h, i: (0, 0)),
            pl.BlockSpec((tq, LANES), lambda bi, h, i: (bi * nq + i, h)),
            pl.BlockSpec((s, LANES), lambda bi, h, i: (bi, h)),
            pl.BlockSpec((s, LANES), lambda bi, h, i: (bi, h)),
        ],
        out_specs=pl.BlockSpec((tq, LANES), lambda bi, h, i: (bi * nq + i, h)),
        out_shape=jax.ShapeDtypeStruct((n, ATTN_W), F32),
        scratch_shapes=[pltpu.VMEM((2, tq, 1), F32), pltpu.VMEM((2, tq, 2 * LANES), F32)],
        compiler_params=_cparams(("parallel", "parallel", "arbitrary")),
        name="attn_prompt",
    )(slopes, lv, sg.reshape(1, VD), qb, kb, vb)


def _scan_rows(a, u):
    tt = a.shape[0]
    row = lax.broadcasted_iota(jnp.int32, a.shape, 0)
    s = 1
    while s < tt:
        keep = row >= s
        a_prev = jnp.where(keep, pltpu.roll(a, s, 0), 1.0)
        u_prev = jnp.where(keep, pltpu.roll(u, s, 0), 0.0)
        u = a * u_prev + u
        a = a * a_prev
        s *= 2
    return a, u


def _lru_gates(xc, wa_ref, ba, wx_ref, bx, exact):
    rs, is_ = [], []
    for nb in range(N_LB):
        xs = xc[:, nb * LB:(nb + 1) * LB]
        if exact:
            rs.append(_dot3(xs, wa_ref[nb]))
            is_.append(_dot3(xs, wx_ref[nb]))
        else:
            xb = xs.astype(BF16)
            rs.append(_dot(xb, wa_ref[nb]))
            is_.append(_dot(xb, wx_ref[nb]))
    r = _sigmoid(jnp.concatenate(rs, axis=-1) + ba)
    i = _sigmoid(jnp.concatenate(is_, axis=-1) + bx)
    return r, i


def _lru_au(xc, r, i, lam):
    log_a = -LRU_C * r * _softplus(-lam)
    a = jnp.exp(log_a)
    u = jnp.sqrt(-_expm1(2.0 * log_a)) * (i * xc)
    return a, u


def _lru_kernel(lx_ref, lg_ref, cw_ref, cb_ref, wa_ref, ba_ref, wx_ref, bx_ref, lam_ref,
                y_ref, conv_ref, hl_ref, xbuf, hc, *, tt):
    t = pl.program_id(1)
    pad = SUBLANES

    @pl.when(t == 0)
    def _():
        xbuf[0:pad, :] = jnp.zeros((pad, LRU_W), F32)
        hc[...] = jnp.zeros_like(hc)

    x = lx_ref[...]
    xbuf[pad:pad + tt, :] = x
    cw = cw_ref[...]
    xc = cb_ref[...] + cw[3:4, :] * x
    for j in range(CONV_W - 1):
        back = CONV_W - 1 - j
        xc = xc + cw[j:j + 1, :] * xbuf[pad - back:pad - back + tt, :]
    xbuf[0:pad, :] = x[tt - pad:tt, :]

    r, i = _lru_gates(xc, wa_ref, ba_ref[...], wx_ref, bx_ref[...], False)
    a, u = _lru_au(xc, r, i, lam_ref[...])
    ca, cu = _scan_rows(a, u)
    hseq = ca * hc[0:1, :] + cu
    hc[0:1, :] = hseq[tt - 1:tt, :]
    y_ref[...] = (hseq * _gelu(lg_ref[...])).astype(y_ref.dtype)

    @pl.when(t == pl.num_programs(1) - 1)
    def _():
        conv_ref[...] = x[tt - (CONV_W - 1):tt, :]
        hl_ref[...] = hseq[tt - 1:tt, :]


def lru_prompt(y, cw, cb, wa, ba, wx, bx, lam, *, b, s, tt):
    n = b * s
    tt = min(tt, s)
    nt = s // tt
    vec = lambda: pl.BlockSpec((1, LRU_W), lambda bi, t: (0, 0))
    mat = lambda: pl.BlockSpec((N_LB, LB, LB), lambda bi, t: (0, 0, 0))
    return pl.pallas_call(
        functools.partial(_lru_kernel, tt=tt),
        grid=(b, nt),
        in_specs=[
            pl.BlockSpec((tt, LRU_W), lambda bi, t: (bi * nt + t, COL_LX)),
            pl.BlockSpec((tt, LRU_W), lambda bi, t: (bi * nt + t, COL_LG)),
            pl.BlockSpec((CONV_W, LRU_W), lambda bi, t: (0, 0)),
            vec(), mat(), vec(), mat(), vec(), vec(),
        ],
        out_specs=[
            pl.BlockSpec((tt, LRU_W), lambda bi, t: (bi * nt + t, 0)),
            pl.BlockSpec((None, CONV_W - 1, LRU_W), lambda bi, t: (bi, 0, 0)),
            pl.BlockSpec((None, 1, LRU_W), lambda bi, t: (bi, 0, 0)),
        ],
        out_shape=[
            jax.ShapeDtypeStruct((n, LRU_W), BF16),
            jax.ShapeDtypeStruct((b, CONV_W - 1, LRU_W), F32),
            jax.ShapeDtypeStruct((b, 1, LRU_W), F32),
        ],
        scratch_shapes=[pltpu.VMEM((tt + SUBLANES, LRU_W), F32), pltpu.VMEM((SUBLANES, LRU_W), F32)],
        compiler_params=_cparams(("parallel", "arbitrary")),
        name="lru_prompt",
    )(y, y, cw, cb.reshape(1, -1), wa.astype(BF16), ba.reshape(1, -1), wx.astype(BF16),
      bx.reshape(1, -1), lam.reshape(1, -1))


H_CHUNK = 64
H_SUB = SUBLANES


def _hgrn_lower_bound(rows, layer):
    mx = rows[0]
    for r in rows[1:]:
        mx = jnp.maximum(mx, r)
    es = [jnp.exp(r - mx) for r in rows]
    tot = es[0]
    for e in es[1:]:
        tot = tot + e
    lb = jnp.zeros_like(mx)
    for d in range(1, layer + 1):
        lb = lb + es[d] / tot
    return lb


def _hgrn_log_f(z, lb):
    log_sig = jnp.minimum(z, 0.0) - jnp.log1p(jnp.exp(-jnp.abs(z)))
    a = jnp.log(lb)
    bb = jnp.log1p(-lb) + log_sig
    return jnp.maximum(a, bb) + jnp.log1p(jnp.exp(-jnp.abs(a - bb)))


def _cumsum_rows(x):
    n = x.shape[0]
    row = lax.broadcasted_iota(jnp.int32, x.shape, 0)
    s = 1
    while s < n:
        x = x + jnp.where(row >= s, pltpu.roll(x, s, 0), 0.0)
        s *= 2
    return x


def _hgrn_chunk(qq, kk, logf, vv, st):
    c = qq.shape[0]
    nsub = c // H_SUB
    bcum = _cumsum_rows(logf)
    row_c = lax.broadcasted_iota(jnp.int32, (c, LANES), 0)
    row_s = lax.broadcasted_iota(jnp.int32, (H_SUB, LANES), 0)
    vb = vv.astype(BF16)

    o = _dot_nt((qq * jnp.exp(bcum)).astype(BF16), st.astype(BF16))

    att_rows = []
    diag_rows = []
    for i in range(nsub):
        r0 = i * H_SUB
        q_r = qq[r0:r0 + H_SUB, :]
        b_r = bcum[r0:r0 + H_SUB, :]
        if i == 0:
            att_rows.append(jnp.zeros((H_SUB, c), F32))
        else:
            beta = bcum[r0 - 1:r0, :]
            qt = q_r * jnp.exp(b_r - beta)
            kt = kk * jnp.exp(jnp.where(row_c < r0, beta - bcum, NEG))
            att_rows.append(_dot_nt(qt.astype(BF16), kt.astype(BF16)))
        k_r = kk[r0:r0 + H_SUB, :]
        v_r = vv[r0:r0 + H_SUB, :]
        od = jnp.zeros((H_SUB, LANES), F32)
        for s in range(H_SUB):
            e = jnp.exp(jnp.where(row_s >= s, b_r - b_r[s:s + 1, :], NEG))
            w = jnp.sum(q_r * k_r[s:s + 1, :] * e, axis=-1, keepdims=True)
            od = od + w * v_r[s:s + 1, :]
        diag_rows.append(od)
    att = jnp.concatenate(att_rows, axis=0)
    o = o + _dot(att.astype(BF16), vb) + jnp.concatenate(diag_rows, axis=0)

    b_last = bcum[c - 1:c, :]
    kd = (kk * jnp.exp(b_last - bcum)).astype(BF16)
    st = st * jnp.exp(b_last) + lax.dot_general(vb, kd, (((0,), (0,)), ((), ())),
                                                 preferred_element_type=F32)
    return o, st


def _hgrn_kernel(hq_ref, hf_ref, hi_ref, hg_ref, lb_ref, g_ref, y_ref, s_ref, st_ref, *, layer, nchunk):
    lb = _hgrn_lower_bound([lb_ref[d:d + 1, :] for d in range(DEPTH)], layer)
    g = g_ref[...]
    st_ref[...] = jnp.zeros_like(st_ref)

    def body(ci, carry):
        r0 = pl.multiple_of(ci * H_CHUNK, H_CHUNK)
        rows = pl.ds(r0, H_CHUNK)
        z = hf_ref[rows, :]
        logf = _hgrn_log_f(z, lb)
        kk = (1.0 - lb) * _sigmoid(-z)
        qq = _silu(hq_ref[rows, :])
        o, st = _hgrn_chunk(qq, kk, logf, hi_ref[rows, :], st_ref[...])
        st_ref[...] = st
        y_ref[rows, :] = (_rms(o, g) * _silu(hg_ref[rows, :])).astype(y_ref.dtype)
        return carry

    lax.fori_loop(0, nchunk, body, 0, unroll=4)
    s_ref[...] = st_ref[...]


def hgrn_prompt(y, lb_all, g, *, b, s, layer):
    n = b * s
    nchunk = s // H_CHUNK
    col = lambda c: pl.BlockSpec((s, LANES), lambda bi, h: (bi, c * N_HH + h))
    return pl.pallas_call(
        functools.partial(_hgrn_kernel, layer=layer, nchunk=nchunk),
        grid=(b, N_HH),
        in_specs=[
            col(COL_HQ), col(COL_HF), col(COL_HI), col(COL_HG),
            pl.BlockSpec((DEPTH, LANES), lambda bi, h: (0, h)),
            pl.BlockSpec((1, HV), lambda bi, h: (0, 0)),
        ],
        out_specs=[
            pl.BlockSpec((s, LANES), lambda bi, h: (bi, h)),
            pl.BlockSpec((None, None, HV, HK), lambda bi, h: (bi, h, 0, 0)),
        ],
        out_shape=[
            jax.ShapeDtypeStruct((n, HGRN_W), BF16),
            jax.ShapeDtypeStruct((b, N_HH, HV, HK), F32),
        ],
        scratch_shapes=[pltpu.VMEM((HV, HK), F32)],
        compiler_params=_cparams(("parallel", "parallel")),
        name="hgrn_prompt",
    )(y, y, y, y, lb_all, g.reshape(1, HV))


def _merge_kernel(x_ref, ya_ref, yl_ref, yh_ref, ga_ref, gl_ref, gh_ref,
                  wa_ref, wl_ref, wh_ref, wo_ref, o_ref, *, exact):
    if exact:
        mm = _dot3
        cast = lambda a: a.astype(F32)
    else:
        mm = _dot
        cast = lambda a: a.astype(BF16)
    m = _sigmoid(ga_ref[...]) * mm(cast(ya_ref[...]), wa_ref[...])
    m = m + _sigmoid(gl_ref[...]) * mm(cast(yl_ref[...]), wl_ref[...])
    m = m + _sigmoid(gh_ref[...]) * mm(cast(yh_ref[...]), wh_ref[...])
    o_ref[...] = x_ref[...] + mm(cast(m), wo_ref[...])


def merge_out(x, y, ya, yl, yh, wa, wl, wh, wo, *, exact, tm):
    n = x.shape[0]
    tm = min(tm, n)
    d = D_MODEL
    row = lambda: pl.BlockSpec((tm, d), lambda i: (i, 0))
    gate = lambda c: pl.BlockSpec((tm, d), lambda i: (i, COL_GT + c))
    wspec = lambda: pl.BlockSpec((d, d), lambda i: (0, 0))
    wdt = F32 if exact else BF16
    return pl.pallas_call(
        functools.partial(_merge_kernel, exact=exact),
        grid=(n // tm,),
        in_specs=[row(), row(), row(), row(), gate(0), gate(1), gate(2), wspec(), wspec(), wspec(), wspec()],
        out_specs=row(),
        out_shape=jax.ShapeDtypeStruct((n, d), F32),
        compiler_params=_cparams(("parallel",)),
        name="merge_out",
    )(x, ya, yl, yh, y, y, y, wa.astype(wdt), wl.astype(wdt), wh.astype(wdt), wo.astype(wdt))


def _topk_rows(x, k):
    out = []
    cur = x
    for _ in range(k):
        mx = jnp.max(cur, axis=0, keepdims=True)
        out.append(mx)
        cur = jnp.where(cur == mx, NEG, cur)
    return out


def _peer_select_kernel(q_ref, kh_ref, sel_ref):
    st = _dot3_nt(kh_ref[...], q_ref[...])
    s1 = st[:N_KEYS, :]
    s2 = st[N_KEYS:, :]
    v1 = _topk_rows(s1, P_TOPK)
    v2 = _topk_rows(s2, P_TOPK)
    v1m = jnp.concatenate(v1, axis=0)
    v2m = jnp.concatenate(v2, axis=0)
    half = P_TOPK // 2
    v2_lo, v2_hi = v2m[:half], v2m[half:]

    row = lax.broadcasted_iota(jnp.int32, v2_lo.shape, 0)
    tiles = [(v1[0] + v2_lo, v2_lo), (v1[0] + v2_hi, v2_hi)]
    for a in range(1, half):
        tiles.append((jnp.where(row < P_TOPK // (a + 1), v1[a] + v2_lo, NEG), v2_lo))
    tail = v1m[half:] + v2[0]

    cur = [c for c, _ in tiles] + [tail]
    tau = None
    for _ in range(P_TOPK):
        mx = cur[0]
        for c in cur[1:]:
            mx = jnp.maximum(mx, c)
        tau = jnp.max(mx, axis=0, keepdims=True)
        cur = [jnp.where(c == tau, NEG, c) for c in cur]

    top = v1[0] + v2[0]
    big = -NEG

    def picked(cand, vals):
        sel = cand >= tau
        mass = jnp.sum(jnp.where(sel, jnp.exp(cand - top), 0.0), axis=0, keepdims=True)
        return mass, jnp.min(jnp.where(sel, vals, big), axis=0, keepdims=True)

    z0, t0 = picked(*tiles[0])
    z1, t1 = picked(*tiles[1])
    z = z0 + z1
    thetas = [jnp.minimum(t0, t1)]
    for a in range(1, half):
        za, ta = picked(*tiles[a + 1])
        z = z + za
        thetas.append(ta)
    sel_tail = tail >= tau
    z = z + jnp.sum(jnp.where(sel_tail, jnp.exp(tail - top), 0.0), axis=0, keepdims=True)
    for a in range(half, P_TOPK):
        thetas.append(jnp.where(sel_tail[a - half:a - half + 1], v2[0], big))

    theta_dense = jnp.full(s1.shape, big, F32)
    for a in range(P_TOPK):
        theta_dense = jnp.where(s1 == v1[a], thetas[a], theta_dense)
    sel_ref[0] = s2
    sel_ref[1] = jnp.exp(s2 - v2[0])
    sel_ref[2] = theta_dense
    sel_ref[3] = jnp.exp(s1 - v1[0]) / z


def peer_select(q, kbd, *, tn):
    n = q.shape[0]
    tn = min(tn, n)
    return pl.pallas_call(
        _peer_select_kernel,
        grid=(n // tn, P_HEADS),
        in_specs=[
            pl.BlockSpec((tn, D_KEY), lambda i, h: (i, h)),
            pl.BlockSpec((None, 2 * N_KEYS, D_KEY), lambda i, h: (h, 0, 0)),
        ],
        out_specs=pl.BlockSpec((None, 4, N_KEYS, tn), lambda i, h: (h, 0, 0, i)),
        out_shape=jax.ShapeDtypeStruct((P_HEADS, 4, N_KEYS, n), F32),
        compiler_params=_cparams(("parallel", "parallel")),
        name="peer_select",
    )(q, kbd)


PEER_II_GROUP = 4


def _peer_dense_kernel(xnt_ref, x_ref, sel_ref, u_ref, vt_ref, o_ref, acc_ref, w_ref, sp_ref, bc_ref, *, ec):
    e = pl.program_id(1)
    tn = w_ref.shape[1]
    n_i1 = ec // N_KEYS
    nt = tn // LANES
    tile = lambda t: slice((t % nt) * LANES, (t % nt + 1) * LANES)
    rot_sp = lambda h, k: h + 2 * k
    rot_bc = lambda ii, h, k: h + 2 * k + ii + 1

    @pl.when(e == 0)
    def _():
        acc_ref[...] = jnp.zeros_like(acc_ref)
        for h in range(P_HEADS):
            for k in range(2):
                for t in range(nt):
                    sp_ref[h, k, :, tile(t + rot_sp(h, k))] = sel_ref[h, k, :, tile(t)]

    for ii in range(n_i1):
        i1 = e * n_i1 + ii
        for h in range(P_HEADS):
            for k in range(2):
                row = sel_ref[h, 2 + k, pl.ds(i1, 1), :]
                for t in range(nt):
                    bc_ref[ii, h, k, :, tile(t + rot_bc(ii, h, k))] = jnp.broadcast_to(row[:, tile(t)],
                                                                                      (SUBLANES, LANES))

    def rows_body(r, carry):
        r0 = pl.multiple_of(r * SUBLANES, SUBLANES)
        rows = pl.ds(r0, SUBLANES)
        for g0 in range(0, n_i1, PEER_II_GROUP):
            group = range(g0, min(g0 + PEER_II_GROUP, n_i1))
            w = [{ii: None for ii in group} for _ in range(nt)]
            for h in range(P_HEADS):
                s2 = [sp_ref[h, 0, rows, tile(t + rot_sp(h, 0))] for t in range(nt)]
                p2 = [sp_ref[h, 1, rows, tile(t + rot_sp(h, 1))] for t in range(nt)]
                for ii in group:
                    for t in range(nt):
                        theta = bc_ref[ii, h, 0, :, tile(t + rot_bc(ii, h, 0))]
                        c1 = bc_ref[ii, h, 1, :, tile(t + rot_bc(ii, h, 1))]
                        term = jnp.where(s2[t] >= theta, p2[t], 0.0) * c1
                        w[t][ii] = term if w[t][ii] is None else w[t][ii] + term
            for t in range(nt):
                for ii in group:
                    w_ref[pl.ds(ii * N_KEYS + r0, SUBLANES), tile(t)] = w[t][ii]
        return carry

    lax.fori_loop(0, N_KEYS // SUBLANES, rows_body, 0)
    a_t = _dot(u_ref[...], xnt_ref[...])
    h_t = (w_ref[...] * _gelu(a_t)).astype(BF16)
    acc_ref[...] += _dot(vt_ref[...], h_t)

    @pl.when(e == pl.num_programs(1) - 1)
    def _():
        o_ref[...] = x_ref[...] + acc_ref[...].T


def peer_dense(xnt, x, sel, u, vt, *, tn, ec):
    n = x.shape[0]
    tn = min(tn, n)
    d = D_MODEL
    return pl.pallas_call(
        functools.partial(_peer_dense_kernel, ec=ec),
        grid=(n // tn, N_EXP // ec),
        in_specs=[
            pl.BlockSpec((d, tn), lambda i, e: (0, i)),
            pl.BlockSpec((tn, d), lambda i, e: (i, 0)),
            pl.BlockSpec((P_HEADS, 4, N_KEYS, tn), lambda i, e: (0, 0, 0, i)),
            pl.BlockSpec((ec, d), lambda i, e: (e, 0)),
            pl.BlockSpec((d, ec), lambda i, e: (0, e)),
        ],
        out_specs=pl.BlockSpec((tn, d), lambda i, e: (i, 0)),
        out_shape=jax.ShapeDtypeStruct((n, d), F32),
        scratch_shapes=[pltpu.VMEM((d, tn), F32), pltpu.VMEM((ec, tn), F32),
                        pltpu.VMEM((P_HEADS, 2, N_KEYS, tn), F32),
                        pltpu.VMEM((ec // N_KEYS, P_HEADS, 2, SUBLANES, tn), F32)],
        compiler_params=_cparams(("parallel", "arbitrary")),
        name="peer_dense",
    )(xnt, x, sel, u, vt)


def _rms_cast_t_kernel(x_ref, g_ref, o_ref):
    o_ref[...] = _rms(x_ref[...], g_ref[...]).T.astype(o_ref.dtype)


def rms_cast_t(x, g, *, tm):
    n, d = x.shape
    tm = min(tm, n)
    return pl.pallas_call(
        _rms_cast_t_kernel,
        grid=(n // tm,),
        in_specs=[pl.BlockSpec((tm, d), lambda i: (i, 0)), pl.BlockSpec((1, d), lambda i: (0, 0))],
        out_specs=pl.BlockSpec((d, tm), lambda i: (0, i)),
        out_shape=jax.ShapeDtypeStruct((d, n), BF16),
        compiler_params=_cparams(("parallel",)),
        name="rms_cast_t",
    )(x, g.reshape(1, d))


def _peer_key_blocks(keys):
    z = jnp.zeros_like(keys[:, 0])
    top = jnp.concatenate([keys[:, 0], z], axis=-1)
    bot = jnp.concatenate([z, keys[:, 1]], axis=-1)
    return jnp.concatenate([top, bot], axis=1)


def peer_ffn(x, g, wq, kbd, u_b, vt_b, *, tn_sel, tn, ec):
    n = x.shape[0]
    pad = (-n) % LANES
    xp = jnp.pad(x, ((0, pad), (0, 0))) if pad else x
    q = rms_matmul(xp, g, wq, exact=True, tm=512, tn=1024)
    sel = peer_select(q, kbd, tn=tn_sel)
    xnt = rms_cast_t(xp, g, tm=512)
    out = peer_dense(xnt, xp, sel, u_b, vt_b, tn=tn, ec=ec)
    return out[:n] if pad else out


def _lam_init(layer):
    return 0.8 - 0.6 * math.exp(-0.3 * layer)


def _alibi_slopes():
    return jnp.exp2(-8.0 * jnp.arange(1, N_AH + 1, dtype=F32) / N_AH)


def layer_prompt(x, l, p, w_in_b, kbd, u_b, vt_b, *, b, s):
    n = b * s
    y = rms_matmul(x, p['norm1_g'][l], w_in_b, exact=False, tm=1024, tn=1024)
    g2 = jnp.tile(p['qk_norm_g'][l], (1, 2))
    qb, kn, kb, vb, vf = qk_norm(y, g2, tm=512)
    ya = attn_prompt(qb, kb, vb, p['diff_lambda'][l], p['subln_g'][l], _alibi_slopes(),
                     b=b, s=s, lam_init=_lam_init(l), tq=1024)
    yl, conv_new, h_last = lru_prompt(y, p['conv_w'][l], p['conv_b'][l], p['rg_wa'][l], p['rg_ba'][l],
                                      p['rg_wx'][l], p['rg_bx'][l], p['rg_lambda'][l], b=b, s=s, tt=256)
    yh, st = hgrn_prompt(y, p['hgrn_lb'], p['hgrn_norm_g'][l], b=b, s=s, layer=l)
    x = merge_out(x, y, ya, yl, yh, p['w_br_attn'][l], p['w_br_lru'][l], p['w_br_hgrn'][l],
                  p['w_out'][l], exact=False, tm=512)
    x = peer_ffn(x, p['norm2_g'][l], p['peer_wq'][l], kbd, u_b, vt_b, tn_sel=256, tn=512, ec=1024)
    k_out = kn.reshape(b, s, N_AH, 2 * HD)
    v_out = vf.reshape(b, s, N_AH, VD)
    s_out = jnp.swapaxes(st, -1, -2)
    return x, (k_out, v_out, conv_new, h_last.reshape(b, LRU_W), s_out)


DEC_PPS = 4
DEC_TOK = 8


def _swap_halves(a):
    return pltpu.roll(a, HD, a.ndim - 1)


def _attn_decode_kernel(pt_ref, q_ref, kn_ref, vn_ref, slope_ref, lv_ref, sg_ref, *refs,
                        pps, past, lam_init):
    k_refs = refs[:pps]
    v_refs = refs[pps:2 * pps]
    o_ref, m_ref, l_ref, acc_ref, s_ref = refs[2 * pps:]
    step = pl.program_id(1)

    @pl.when(step == 0)
    def _():
        m_ref[...] = jnp.full(m_ref.shape, NEG, F32)
        l_ref[...] = jnp.zeros_like(l_ref)
        acc_ref[...] = jnp.zeros_like(acc_ref)

    q = q_ref[...]
    seg = _seg_matrix()
    slope = slope_ref[...]
    tbias = slope[None] * lax.broadcasted_iota(jnp.int32, (PAGE_SIZE, N_AH, LANES), 0).astype(F32)

    def rescale(m_new):
        alpha = jnp.exp(m_ref[...] - m_new)
        l_ref[...] = l_ref[...] * alpha
        acc_ref[0] = acc_ref[0] * alpha
        acc_ref[1] = acc_ref[1] * _swap_halves(alpha)
        m_ref[...] = m_new

    offs = []
    m_new = m_ref[...]
    for i in range(pps):
        prod = (k_refs[i][...] * q[None]).reshape(PAGE_SIZE * N_AH, LANES).astype(BF16)
        s = _dot(prod, seg).reshape(PAGE_SIZE, N_AH, LANES) + tbias
        s_ref[i] = s
        offs.append(slope * ((step * pps + i) * PAGE_SIZE - past).astype(F32))
        m_new = jnp.maximum(m_new, jnp.max(s, axis=0) + offs[i])
    rescale(m_new)
    shift = [m_new - off for off in offs]

    def tokens_body(g, carry):
        l_a, a_s, a_x = carry
        rows = pl.ds(pl.multiple_of(g * DEC_TOK, DEC_TOK), DEC_TOK)
        for i in range(pps):
            pw = jnp.exp(s_ref[i, rows] - shift[i])
            vals = v_refs[i][rows]
            l_a = l_a + jnp.sum(pw, axis=0)
            a_s = a_s + jnp.sum(pw * vals, axis=0)
            a_x = a_x + jnp.sum(_swap_halves(pw) * vals, axis=0)
        return l_a, a_s, a_x

    zero = jnp.zeros((N_AH, LANES), F32)
    l_a, a_s, a_x = lax.fori_loop(0, PAGE_SIZE // DEC_TOK, tokens_body, (zero, zero, zero), unroll=4)
    l_ref[...] += l_a
    acc_ref[0] += a_s
    acc_ref[1] += a_x

    @pl.when(step == pl.num_programs(1) - 1)
    def _():
        sn = _dot2_exact_rhs(kn_ref[...] * q, seg)
        rescale(jnp.maximum(m_ref[...], sn))
        pn = jnp.exp(sn - m_ref[...])
        vn = vn_ref[...]
        l_ref[...] += pn
        acc_ref[0] += pn * vn
        acc_ref[1] += _swap_halves(pn) * vn
        first = lax.broadcasted_iota(jnp.int32, (N_AH, LANES), 1) < HD
        l = l_ref[...]
        l_x = _swap_halves(l)
        a_s = acc_ref[0]
        a_x = acc_ref[1]
        o0 = jnp.where(first, a_s, a_x) / jnp.where(first, l, l_x)
        o1 = jnp.where(first, a_x, a_s) / jnp.where(first, l_x, l)
        lam = _diff_lambda(lv_ref[...], lam_init)
        o = o0 - lam * o1
        o_ref[...] = _rms(o, sg_ref[...]) * (1.0 - lam_init)


def attn_decode(page_table, q, kn, vn, cache_k, cache_v, lv, sg, slopes, *, layer, lam_init):
    bd, n_pages = page_table.shape
    pps = DEC_PPS
    past = n_pages * PAGE_SIZE
    tok = lambda: pl.BlockSpec((None, N_AH, LANES), lambda b, st, pt: (b, 0, 0))
    page = lambda i: pl.BlockSpec((None, None, PAGE_SIZE, N_AH, LANES),
                                  lambda b, st, pt: (layer, pt[b, st * pps + i], 0, 0, 0))
    grid_spec = pltpu.PrefetchScalarGridSpec(
        num_scalar_prefetch=1,
        grid=(bd, n_pages // pps),
        in_specs=[
            tok(), tok(), tok(),
            pl.BlockSpec((N_AH, LANES), lambda b, st, pt: (0, 0)),
            pl.BlockSpec((4, HD), lambda b, st, pt: (0, 0)),
            pl.BlockSpec((1, VD), lambda b, st, pt: (0, 0)),
        ] + [page(i) for i in range(pps)] + [page(i) for i in range(pps)],
        out_specs=pl.BlockSpec((None, N_AH, LANES), lambda b, st, pt: (b, 0, 0)),
        scratch_shapes=[pltpu.VMEM((N_AH, LANES), F32), pltpu.VMEM((N_AH, LANES), F32),
                        pltpu.VMEM((2, N_AH, LANES), F32),
                        pltpu.VMEM((pps, PAGE_SIZE, N_AH, LANES), F32)],
    )
    slope_tile = jnp.broadcast_to(slopes[:, None], (N_AH, LANES))
    return pl.pallas_call(
        functools.partial(_attn_decode_kernel, pps=pps, past=past, lam_init=lam_init),
        grid_spec=grid_spec,
        out_shape=jax.ShapeDtypeStruct((bd, N_AH, LANES), F32),
        compiler_params=_cparams(("parallel", "arbitrary")),
        name="attn_decode",
    )(page_table, q, kn, vn, slope_tile, lv, sg.reshape(1, VD), *([cache_k] * pps), *([cache_v] * pps))


def _lru_step_kernel(lx_ref, lg_ref, buf_ref, h0_ref, cw_ref, cb_ref, wa_ref, ba_ref, wx_ref, bx_ref,
                     lam_ref, yl_ref, conv_ref, h_ref):
    x = lx_ref[...]
    cw = cw_ref[...]
    xc = cb_ref[...] + cw[CONV_W - 1:CONV_W, :] * x
    for j in range(CONV_W - 1):
        xc = xc + cw[j:j + 1, :] * buf_ref[j]
    r, i = _lru_gates(xc, wa_ref, ba_ref[...], wx_ref, bx_ref[...], True)
    a, u = _lru_au(xc, r, i, lam_ref[...])
    h = a * h0_ref[...] + u
    yl_ref[...] = h * _gelu(lg_ref[...])
    for j in range(CONV_W - 2):
        conv_ref[j] = buf_ref[j + 1]
    conv_ref[CONV_W - 2] = x
    h_ref[...] = h


def lru_step(y, buf_t, h0, cw, cb, wa, ba, wx, bx, lam):
    bd = y.shape[0]
    full = lambda shape: pl.BlockSpec(shape, lambda i: (0,) * len(shape))
    return pl.pallas_call(
        _lru_step_kernel,
        grid=(1,),
        in_specs=[
            pl.BlockSpec((bd, LRU_W), lambda i: (0, COL_LX)),
            pl.BlockSpec((bd, LRU_W), lambda i: (0, COL_LG)),
            full((CONV_W - 1, bd, LRU_W)), full((bd, LRU_W)), full((CONV_W, LRU_W)), full((1, LRU_W)),
            full((N_LB, LB, LB)), full((1, LRU_W)), full((N_LB, LB, LB)), full((1, LRU_W)), full((1, LRU_W)),
        ],
        out_specs=[full((bd, LRU_W)), full((CONV_W - 1, bd, LRU_W)), full((bd, LRU_W))],
        out_shape=[
            jax.ShapeDtypeStruct((bd, LRU_W), F32),
            jax.ShapeDtypeStruct((CONV_W - 1, bd, LRU_W), F32),
            jax.ShapeDtypeStruct((bd, LRU_W), F32),
        ],
        compiler_params=_cparams(("arbitrary",)),
        name="lru_step",
    )(y, y, buf_t, h0, cw, cb.reshape(1, -1), wa, ba.reshape(1, -1), wx, bx.reshape(1, -1), lam.reshape(1, -1))


def _hgrn_step_kernel(hq_ref, hf_ref, hi_ref, hg_ref, lb_ref, g_ref, s0_ref, y_ref, s1_ref, *, layer):
    lb = _hgrn_lower_bound([lb_ref[d] for d in range(DEPTH)], layer)
    z = hf_ref[...]
    f = jnp.exp(_hgrn_log_f(z, lb))
    kk = (1.0 - lb) * _sigmoid(-z)
    qq = _silu(hq_ref[...])
    vv = hi_ref[...]
    pad = jnp.zeros((HK - N_HH, LANES), F32)
    cols = lambda a: jnp.concatenate([a, pad], axis=0).T
    q_t, f_t, k_t = cols(qq), cols(f), cols(kk)
    row = lax.broadcasted_iota(jnp.int32, (N_HH, LANES), 0)
    o = jnp.zeros((N_HH, LANES), F32)
    for h in range(N_HH):
        s1 = s0_ref[h] * f_t[:, h:h + 1] + k_t[:, h:h + 1] * vv[h:h + 1, :]
        s1_ref[h] = s1
        oh = jnp.sum(q_t[:, h:h + 1] * s1, axis=0, keepdims=True)
        o = jnp.where(row == h, oh, o)
    y_ref[...] = _rms(o, g_ref[...]) * _silu(hg_ref[...])


def hgrn_step(y3, lb3, g, state, *, layer):
    bd = y3.shape[0]
    col = lambda c: pl.BlockSpec((None, N_HH, LANES), lambda b: (b, c, 0))
    return pl.pallas_call(
        functools.partial(_hgrn_step_kernel, layer=layer),
        grid=(bd,),
        in_specs=[
            col(COL_HQ), col(COL_HF), col(COL_HI), col(COL_HG),
            pl.BlockSpec((DEPTH, N_HH, LANES), lambda b: (0, 0, 0)),
            pl.BlockSpec((1, HV), lambda b: (0, 0)),
            pl.BlockSpec((None, None, N_HH, HK, HV), lambda b: (layer, b, 0, 0, 0)),
        ],
        out_specs=[
            pl.BlockSpec((None, N_HH, LANES), lambda b: (b, 0, 0)),
            pl.BlockSpec((None, N_HH, HK, HV), lambda b: (b, 0, 0, 0)),
        ],
        out_shape=[
            jax.ShapeDtypeStruct((bd, N_HH, LANES), F32),
            jax.ShapeDtypeStruct((bd, N_HH, HK, HV), F32),
        ],
        compiler_params=_cparams(("parallel",)),
        name="hgrn_step",
    )(y3, y3, y3, y3, lb3, g.reshape(1, HV), state)


def layer_sample(x, l, p, cache_k, cache_v, page_table, state_conv, state_lru, state_hgrn, kbd, u_b, vt_b):
    bd = x.shape[0]
    y = rms_matmul(x, p['norm1_g'][l], p['w_in'][l], exact=True, tm=bd, tn=1024)
    g2 = jnp.tile(p['qk_norm_g'][l], (1, 2))
    qf, kn, _, _, v_new = qk_norm(y, g2, tm=bd, q_dtype=F32)
    tok = lambda a: a.reshape(bd, N_AH, LANES)
    ya = attn_decode(page_table, tok(qf), tok(kn), tok(v_new), cache_k, cache_v, p['diff_lambda'][l],
                     p['subln_g'][l], _alibi_slopes(), layer=l, lam_init=_lam_init(l))
    yl, conv_t, h_new = lru_step(y, jnp.swapaxes(state_conv[l], 0, 1), state_lru[l], p['conv_w'][l],
                                 p['conv_b'][l], p['rg_wa'][l], p['rg_ba'][l], p['rg_wx'][l], p['rg_bx'][l],
                                 p['rg_lambda'][l])
    yh, s1 = hgrn_step(y.reshape(bd, IN_W // LANES, LANES), p['hgrn_lb'].reshape(DEPTH, N_HH, LANES),
                       p['hgrn_norm_g'][l], state_hgrn, layer=l)
    x = merge_out(x, y, ya.reshape(bd, ATTN_W), yl, yh.reshape(bd, HGRN_W), p['w_br_attn'][l],
                  p['w_br_lru'][l], p['w_br_hgrn'][l], p['w_out'][l], exact=True, tm=bd)
    x = peer_ffn(x, p['norm2_g'][l], p['peer_wq'][l], kbd, u_b, vt_b, tn_sel=256, tn=512, ec=1024)
    return x, (kn.reshape(bd, 1, N_AH, 2 * HD), v_new.reshape(bd, 1, N_AH, VD), jnp.swapaxes(conv_t, 0, 1),
               h_new, s1)


def kernel(x_prompt, x_sample, cache_k, cache_v, state_conv, state_lru, state_hgrn, page_table,
           norm1_g, norm2_g, w_in, qk_norm_g, diff_lambda, subln_g, conv_w, conv_b,
           rg_wa, rg_ba, rg_wx, rg_bx, rg_lambda, hgrn_lb, hgrn_norm_g,
           w_br_attn, w_br_lru, w_br_hgrn, w_out, peer_wq, peer_keys, peer_u, peer_v):
    p = dict(norm1_g=norm1_g, norm2_g=norm2_g, w_in=w_in, qk_norm_g=qk_norm_g, diff_lambda=diff_lambda,
             subln_g=subln_g, conv_w=conv_w, conv_b=conv_b, rg_wa=rg_wa, rg_ba=rg_ba, rg_wx=rg_wx,
             rg_bx=rg_bx, rg_lambda=rg_lambda, hgrn_lb=hgrn_lb, hgrn_norm_g=hgrn_norm_g,
             w_br_attn=w_br_attn, w_br_lru=w_br_lru, w_br_hgrn=w_br_hgrn, w_out=w_out,
             peer_wq=peer_wq, peer_keys=peer_keys, peer_u=peer_u, peer_v=peer_v)
    b, s, _ = x_prompt.shape
    bd = x_sample.shape[0]
    xp = x_prompt.reshape(b * s, D_MODEL)
    xs = x_sample.reshape(bd, D_MODEL)
    st_p, st_s = [], []
    for l in range(DEPTH):
        w_in_b = w_in[l].astype(BF16)
        kbd = _peer_key_blocks(peer_keys[l])
        u_b = peer_u[l].astype(BF16)
        vt_b = peer_v[l].T.astype(BF16)
        xp, sp = layer_prompt(xp, l, p, w_in_b, kbd, u_b, vt_b, b=b, s=s)
        xs, ss = layer_sample(xs, l, p, cache_k, cache_v, page_table, state_conv, state_lru, state_hgrn,
                              kbd, u_b, vt_b)
        st_p.append(sp)
        st_s.append(ss)
    k_p, v_p, conv_p, lru_p, hgrn_p = [jnp.stack(a) for a in zip(*st_p)]
    k_s, v_s, conv_s, lru_s, hgrn_s = [jnp.stack(a) for a in zip(*st_s)]
    return (xp.reshape(b, s, D_MODEL), xs.reshape(bd, 1, D_MODEL), k_p, v_p, conv_p, lru_p, hgrn_p,
            k_s, v_s, conv_s, lru_s, hgrn_s)
```

```python
import functools
import math

import jax
import jax.numpy as jnp
from jax import lax
from jax.experimental import pallas as pl
from jax.experimental.pallas import tpu as pltpu

F32 = jnp.float32
BF16 = jnp.bfloat16

D_MODEL = 1024
DEPTH = 2
PAGE_SIZE = 128
N_AH = 8
HD = 64
VD = 2 * HD
ATTN_W = N_AH * VD
LRU_W = 1024
N_LB = 8
LB = LRU_W // N_LB
CONV_W = 4
LRU_C = 8.0
N_HH = 8
HK = 128
HV = 128
HGRN_W = N_HH * HV
P_HEADS = 8
N_KEYS = 128
N_EXP = N_KEYS * N_KEYS
D_KEY = 128
P_TOPK = 16
EPS = 1e-6
IN_W = 12 * D_MODEL

COL_Q, COL_K, COL_V, COL_LX, COL_LG, COL_HQ, COL_HF, COL_HI, COL_HG, COL_GT = 0, 1, 2, 3, 4, 5, 6, 7, 8, 9

LANES = 128
SUBLANES = 8
VMEM_LIMIT = 56 * 1024 * 1024

NEG = -1e30
SQRT_HALF = 0.7071067811865476


def _cparams(sem, flags=None):
    return pltpu.CompilerParams(dimension_semantics=sem, vmem_limit_bytes=VMEM_LIMIT, flags=flags)


def _split(a):
    hi = a.astype(BF16)
    lo = (a - hi.astype(F32)).astype(BF16)
    return hi, lo


def _dot(a, b):
    return jnp.dot(a, b, preferred_element_type=F32)


def _dot_nt(a, b):
    return lax.dot_general(a, b, (((1,), (1,)), ((), ())), preferred_element_type=F32)


def _dot3(a, b):
    ah, al = _split(a)
    bh, bl = _split(b)
    return _dot(ah, bh) + _dot(al, bh) + _dot(ah, bl)


def _dot3_nt(a, b):
    ah, al = _split(a)
    bh, bl = _split(b)
    return _dot_nt(ah, bh) + _dot_nt(al, bh) + _dot_nt(ah, bl)


def _dot2_exact_rhs(a, b_bf16):
    ah, al = _split(a)
    return _dot(ah, b_bf16) + _dot(al, b_bf16)


def _sigmoid(x):
    return 1.0 / (1.0 + jnp.exp(-x))


def _gelu(x):
    return 0.5 * x * (1.0 + lax.erf(x * SQRT_HALF))


def _silu(x):
    return x * _sigmoid(x)


def _softplus(x):
    return jnp.maximum(x, 0.0) + jnp.log1p(jnp.exp(-jnp.abs(x)))


def _expm1(x):
    u = jnp.exp(x)
    um1 = u - 1.0
    corrected = um1 * x / jnp.log(u)
    return jnp.where(um1 == 0.0, x, jnp.where(um1 == -1.0, -1.0, corrected))


def _rms(x, g):
    ms = jnp.mean(x * x, axis=-1, keepdims=True)
    return x * lax.rsqrt(ms + EPS) * g


def _rms_mm_kernel(x_ref, g_ref, w_ref, o_ref, xn_ref, *, exact):
    @pl.when(pl.program_id(1) == 0)
    def _():
        xn_ref[...] = _rms(x_ref[...], g_ref[...]).astype(xn_ref.dtype)

    if exact:
        o_ref[...] = _dot3(xn_ref[...], w_ref[...])
    else:
        o_ref[...] = _dot(xn_ref[...], w_ref[...])


def rms_matmul(x, g, w, *, exact, tm, tn):
    n, d = x.shape
    wd = w.shape[1]
    tm = min(tm, n)
    tn = min(tn, wd)
    return pl.pallas_call(
        functools.partial(_rms_mm_kernel, exact=exact),
        grid=(n // tm, wd // tn),
        in_specs=[
            pl.BlockSpec((tm, d), lambda i, j: (i, 0)),
            pl.BlockSpec((1, d), lambda i, j: (0, 0)),
            pl.BlockSpec((d, tn), lambda i, j: (0, j)),
        ],
        out_specs=pl.BlockSpec((tm, tn), lambda i, j: (i, j)),
        out_shape=jax.ShapeDtypeStruct((n, wd), F32),
        scratch_shapes=[pltpu.VMEM((tm, d), F32 if exact else BF16)],
        compiler_params=_cparams(("parallel", "arbitrary")),
        name="rms_matmul",
    )(x, g.reshape(1, d), w)


def _seg_matrix():
    r = lax.broadcasted_iota(jnp.int32, (LANES, LANES), 0) // HD
    c = lax.broadcasted_iota(jnp.int32, (LANES, LANES), 1) // HD
    return jnp.where(r == c, 1.0, 0.0).astype(BF16)


def _subhead_norm(x, g, seg):
    ss = _dot2_exact_rhs(x * x, seg)
    return x * lax.rsqrt(ss * (1.0 / HD) + EPS) * g


def _qk_norm_kernel(q_ref, k_ref, v_ref, g_ref, qb_ref, kn_ref, kb_ref, vb_ref, vf_ref):
    seg = _seg_matrix()
    gq = g_ref[0:1, :]
    gk = g_ref[1:2, :]
    for h in range(N_AH):
        sl = slice(h * LANES, (h + 1) * LANES)
        qn = _subhead_norm(q_ref[:, sl], gq, seg)
        kn = _subhead_norm(k_ref[:, sl], gk, seg)
        qb_ref[:, sl] = (qn * (HD ** -0.5)).astype(qb_ref.dtype)
        kn_ref[:, sl] = kn
        kb_ref[:, sl] = kn.astype(BF16)
    v = v_ref[...]
    vb_ref[...] = v.astype(BF16)
    vf_ref[...] = v


def qk_norm(y, g2, *, tm, q_dtype=BF16):
    n = y.shape[0]
    tm = min(tm, n)
    w = ATTN_W
    col = lambda c: pl.BlockSpec((tm, w), lambda i: (i, c))
    out = pl.BlockSpec((tm, w), lambda i: (i, 0))
    return pl.pallas_call(
        _qk_norm_kernel,
        grid=(n // tm,),
        in_specs=[col(COL_Q), col(COL_K), col(COL_V), pl.BlockSpec((2, LANES), lambda i: (0, 0))],
        out_specs=[out, out, out, out, out],
        out_shape=[
            jax.ShapeDtypeStruct((n, w), q_dtype),
            jax.ShapeDtypeStruct((n, w), F32),
            jax.ShapeDtypeStruct((n, w), BF16),
            jax.ShapeDtypeStruct((n, w), BF16),
            jax.ShapeDtypeStruct((n, w), F32),
        ],
        compiler_params=_cparams(("parallel",)),
        name="qk_norm",
    )(y, y, y, g2)


def _diff_lambda(lv, lam_init):
    t1 = jnp.sum(lv[0:1, :] * lv[1:2, :], axis=-1, keepdims=True)
    t2 = jnp.sum(lv[2:3, :] * lv[3:4, :], axis=-1, keepdims=True)
    return jnp.exp(t1) - jnp.exp(t2) + lam_init


def _attn_kernel(slope_ref, lv_ref, sg_ref, q_ref, k_ref, v_ref, o_ref, m_ref, acc_ref, *, tq, lam_init):
    h = pl.program_id(1)
    qi = pl.program_id(2)
    slope = slope_ref[h]
    q = q_ref[...]
    first = lax.broadcasted_iota(jnp.int32, (tq, LANES), 1) < HD
    col = lax.broadcasted_iota(jnp.int32, (1, tq), 1)
    ones = jnp.ones((tq, LANES), BF16)
    m_ref[...] = jnp.full(m_ref.shape, NEG, F32)
    acc_ref[...] = jnp.zeros_like(acc_ref)

    def update(c, s, v1):
        m_old = m_ref[c]
        m_new = jnp.maximum(m_old, jnp.max(s, axis=-1, keepdims=True))
        p = jnp.exp(s - m_new).astype(BF16)
        acc_ref[c] = acc_ref[c] * jnp.exp(m_old - m_new) + _dot(p, v1)
        m_ref[c] = m_new

    def block(j, masked):
        start = pl.multiple_of(j * tq, tq)
        kk = k_ref[pl.ds(start, tq), :]
        v1 = jnp.concatenate([v_ref[pl.ds(start, tq), :], ones], axis=-1)
        zero = jnp.zeros_like(kk)
        bias = slope * ((j - qi) * tq + col).astype(F32)
        for c in range(2):
            kc = jnp.where(first, kk, zero) if c == 0 else jnp.where(first, zero, kk)
            s = _dot_nt(q, kc) + bias
            if masked:
                row2 = lax.broadcasted_iota(jnp.int32, (tq, tq), 0)
                col2 = lax.broadcasted_iota(jnp.int32, (tq, tq), 1)
                s = jnp.where(col2 <= row2, s, NEG)
            update(c, s, v1)

    def body(j, carry):
        block(j, False)
        return carry

    lax.fori_loop(0, qi, body, 0)
    block(qi, True)

    lam = _diff_lambda(lv_ref[...], lam_init)
    r0 = acc_ref[0]
    r1 = acc_ref[1]
    o = r0[:, :LANES] / r0[:, LANES:] - lam * (r1[:, :LANES] / r1[:, LANES:])
    o_ref[...] = _rms(o, sg_ref[...]) * (1.0 - lam_init)


def attn_prompt(qb, kb, vb, lv, sg, slopes, *, b, s, lam_init, tq):
    n = b * s
    tq = min(tq, s)
    nq = s // tq
    return pl.pallas_call(
        functools.partial(_attn_kernel, tq=tq, lam_init=lam_init),
        grid=(b, N_AH, nq),
        in_specs=[
            pl.BlockSpec(memory_space=pltpu.SMEM),
            pl.BlockSpec((4, HD), lambda bi, h, i: (0, 0)),
            pl.BlockSpec((1, VD), lambda bi, h, i: (0, 0)),
            pl.BlockSpec((tq, LANES), lambda bi, h, i: (bi * nq + i, h)),
            pl.BlockSpec((s, LANES), lambda bi, h, i: (bi, h)),
            pl.BlockSpec((s, LANES), lambda bi, h, i: (bi, h)),
        ],
        out_specs=pl.BlockSpec((tq, LANES), lambda bi, h, i: (bi * nq + i, h)),
        out_shape=jax.ShapeDtypeStruct((n, ATTN_W), F32),
        scratch_shapes=[pltpu.VMEM((2, tq, 1), F32), pltpu.VMEM((2, tq, 2 * LANES), F32)],
        compiler_params=_cparams(("parallel", "parallel", "arbitrary")),
        name="attn_prompt",
    )(slopes, lv, sg.reshape(1, VD), qb, kb, vb)


def _scan_rows(a, u):
    tt = a.shape[0]
    row = lax.broadcasted_iota(jnp.int32, a.shape, 0)
    s = 1
    while s < tt:
        keep = row >= s
        a_prev = jnp.where(keep, pltpu.roll(a, s, 0), 1.0)
        u_prev = jnp.where(keep, pltpu.roll(u, s, 0), 0.0)
        u = a * u_prev + u
        a = a * a_prev
        s *= 2
    return a, u


def _lru_gates(xc, wa_ref, ba, wx_ref, bx, exact):
    rs, is_ = [], []
    for nb in range(N_LB):
        xs = xc[:, nb * LB:(nb + 1) * LB]
        if exact:
            rs.append(_dot3(xs, wa_ref[nb]))
            is_.append(_dot3(xs, wx_ref[nb]))
        else:
            xb = xs.astype(BF16)
            rs.append(_dot(xb, wa_ref[nb]))
            is_.append(_dot(xb, wx_ref[nb]))
    r = _sigmoid(jnp.concatenate(rs, axis=-1) + ba)
    i = _sigmoid(jnp.concatenate(is_, axis=-1) + bx)
    return r, i


def _lru_au(xc, r, i, lam):
    log_a = -LRU_C * r * _softplus(-lam)
    a = jnp.exp(log_a)
    u = jnp.sqrt(-_expm1(2.0 * log_a)) * (i * xc)
    return a, u


def _lru_kernel(lx_ref, lg_ref, cw_ref, cb_ref, wa_ref, ba_ref, wx_ref, bx_ref, lam_ref,
                y_ref, conv_ref, hl_ref, xbuf, hc, *, tt):
    t = pl.program_id(1)
    pad = SUBLANES

    @pl.when(t == 0)
    def _():
        xbuf[0:pad, :] = jnp.zeros((pad, LRU_W), F32)
        hc[...] = jnp.zeros_like(hc)

    x = lx_ref[...]
    xbuf[pad:pad + tt, :] = x
    cw = cw_ref[...]
    xc = cb_ref[...] + cw[3:4, :] * x
    for j in range(CONV_W - 1):
        back = CONV_W - 1 - j
        xc = xc + cw[j:j + 1, :] * xbuf[pad - back:pad - back + tt, :]
    xbuf[0:pad, :] = x[tt - pad:tt, :]

    r, i = _lru_gates(xc, wa_ref, ba_ref[...], wx_ref, bx_ref[...], False)
    a, u = _lru_au(xc, r, i, lam_ref[...])
    ca, cu = _scan_rows(a, u)
    hseq = ca * hc[0:1, :] + cu
    hc[0:1, :] = hseq[tt - 1:tt, :]
    y_ref[...] = (hseq * _gelu(lg_ref[...])).astype(y_ref.dtype)

    @pl.when(t == pl.num_programs(1) - 1)
    def _():
        conv_ref[...] = x[tt - (CONV_W - 1):tt, :]
        hl_ref[...] = hseq[tt - 1:tt, :]


def lru_prompt(y, cw, cb, wa, ba, wx, bx, lam, *, b, s, tt):
    n = b * s
    tt = min(tt, s)
    nt = s // tt
    vec = lambda: pl.BlockSpec((1, LRU_W), lambda bi, t: (0, 0))
    mat = lambda: pl.BlockSpec((N_LB, LB, LB), lambda bi, t: (0, 0, 0))
    return pl.pallas_call(
        functools.partial(_lru_kernel, tt=tt),
        grid=(b, nt),
        in_specs=[
            pl.BlockSpec((tt, LRU_W), lambda bi, t: (bi * nt + t, COL_LX)),
            pl.BlockSpec((tt, LRU_W), lambda bi, t: (bi * nt + t, COL_LG)),
            pl.BlockSpec((CONV_W, LRU_W), lambda bi, t: (0, 0)),
            vec(), mat(), vec(), mat(), vec(), vec(),
        ],
        out_specs=[
            pl.BlockSpec((tt, LRU_W), lambda bi, t: (bi * nt + t, 0)),
            pl.BlockSpec((None, CONV_W - 1, LRU_W), lambda bi, t: (bi, 0, 0)),
            pl.BlockSpec((None, 1, LRU_W), lambda bi, t: (bi, 0, 0)),
        ],
        out_shape=[
            jax.ShapeDtypeStruct((n, LRU_W), BF16),
            jax.ShapeDtypeStruct((b, CONV_W - 1, LRU_W), F32),
            jax.ShapeDtypeStruct((b, 1, LRU_W), F32),
        ],
        scratch_shapes=[pltpu.VMEM((tt + SUBLANES, LRU_W), F32), pltpu.VMEM((SUBLANES, LRU_W), F32)],
        compiler_params=_cparams(("parallel", "arbitrary")),
        name="lru_prompt",
    )(y, y, cw, cb.reshape(1, -1), wa.astype(BF16), ba.reshape(1, -1), wx.astype(BF16),
      bx.reshape(1, -1), lam.reshape(1, -1))


H_CHUNK = 64
H_SUB = SUBLANES


def _hgrn_lower_bound(rows, layer):
    mx = rows[0]
    for r in rows[1:]:
        mx = jnp.maximum(mx, r)
    es = [jnp.exp(r - mx) for r in rows]
    tot = es[0]
    for e in es[1:]:
        tot = tot + e
    lb = jnp.zeros_like(mx)
    for d in range(1, layer + 1):
        lb = lb + es[d] / tot
    return lb


def _hgrn_log_f(z, lb):
    log_sig = jnp.minimum(z, 0.0) - jnp.log1p(jnp.exp(-jnp.abs(z)))
    a = jnp.log(lb)
    bb = jnp.log1p(-lb) + log_sig
    return jnp.maximum(a, bb) + jnp.log1p(jnp.exp(-jnp.abs(a - bb)))


def _cumsum_rows(x):
    n = x.shape[0]
    row = lax.broadcasted_iota(jnp.int32, x.shape, 0)
    s = 1
    while s < n:
        x = x + jnp.where(row >= s, pltpu.roll(x, s, 0), 0.0)
        s *= 2
    return x


def _hgrn_chunk(qq, kk, logf, vv, st):
    c = qq.shape[0]
    nsub = c // H_SUB
    bcum = _cumsum_rows(logf)
    row_c = lax.broadcasted_iota(jnp.int32, (c, LANES), 0)
    row_s = lax.broadcasted_iota(jnp.int32, (H_SUB, LANES), 0)
    vb = vv.astype(BF16)

    o = _dot_nt((qq * jnp.exp(bcum)).astype(BF16), st.astype(BF16))

    att_rows = []
    diag_rows = []
    for i in range(nsub):
        r0 = i * H_SUB
        q_r = qq[r0:r0 + H_SUB, :]
        b_r = bcum[r0:r0 + H_SUB, :]
        if i == 0:
            att_rows.append(jnp.zeros((H_SUB, c), F32))
        else:
            beta = bcum[r0 - 1:r0, :]
            qt = q_r * jnp.exp(b_r - beta)
            kt = kk[:r0, :] * jnp.exp(beta - bcum[:r0, :])
            kt = jnp.concatenate([kt, jnp.zeros((c - r0, LANES), F32)], axis=0)
            att_rows.append(_dot_nt(qt.astype(BF16), kt.astype(BF16)))
        k_r = kk[r0:r0 + H_SUB, :]
        v_r = vv[r0:r0 + H_SUB, :]
        od = jnp.zeros((H_SUB, LANES), F32)
        for s in range(H_SUB):
            e = jnp.exp(jnp.where(row_s >= s, b_r - b_r[s:s + 1, :], NEG))
            w = jnp.sum(q_r * k_r[s:s + 1, :] * e, axis=-1, keepdims=True)
            od = od + w * v_r[s:s + 1, :]
        diag_rows.append(od)
    att = jnp.concatenate(att_rows, axis=0)
    o = o + _dot(att.astype(BF16), vb) + jnp.concatenate(diag_rows, axis=0)

    b_last = bcum[c - 1:c, :]
    kd = (kk * jnp.exp(b_last - bcum)).astype(BF16)
    st = st * jnp.exp(b_last) + lax.dot_general(vb, kd, (((0,), (0,)), ((), ())),
                                                 preferred_element_type=F32)
    return o, st


def _hgrn_kernel(hq_ref, hf_ref, hi_ref, hg_ref, lb_ref, g_ref, y_ref, s_ref, st_ref, *, layer, nchunk):
    lb = _hgrn_lower_bound([lb_ref[d:d + 1, :] for d in range(DEPTH)], layer)
    g = g_ref[...]
    st_ref[...] = jnp.zeros_like(st_ref)

    def body(ci, carry):
        r0 = pl.multiple_of(ci * H_CHUNK, H_CHUNK)
        rows = pl.ds(r0, H_CHUNK)
        z = hf_ref[rows, :]
        logf = _hgrn_log_f(z, lb)
        kk = (1.0 - lb) * _sigmoid(-z)
        qq = _silu(hq_ref[rows, :])
        o, st = _hgrn_chunk(qq, kk, logf, hi_ref[rows, :], st_ref[...])
        st_ref[...] = st
        y_ref[rows, :] = (_rms(o, g) * _silu(hg_ref[rows, :])).astype(y_ref.dtype)
        return carry

    lax.fori_loop(0, nchunk, body, 0, unroll=4)
    s_ref[...] = st_ref[...]


def hgrn_prompt(y, lb_all, g, *, b, s, layer):
    n = b * s
    nchunk = s // H_CHUNK
    col = lambda c: pl.BlockSpec((s, LANES), lambda bi, h: (bi, c * N_HH + h))
    return pl.pallas_call(
        functools.partial(_hgrn_kernel, layer=layer, nchunk=nchunk),
        grid=(b, N_HH),
        in_specs=[
            col(COL_HQ), col(COL_HF), col(COL_HI), col(COL_HG),
            pl.BlockSpec((DEPTH, LANES), lambda bi, h: (0, h)),
            pl.BlockSpec((1, HV), lambda bi, h: (0, 0)),
        ],
        out_specs=[
            pl.BlockSpec((s, LANES), lambda bi, h: (bi, h)),
            pl.BlockSpec((None, None, HV, HK), lambda bi, h: (bi, h, 0, 0)),
        ],
        out_shape=[
            jax.ShapeDtypeStruct((n, HGRN_W), BF16),
            jax.ShapeDtypeStruct((b, N_HH, HV, HK), F32),
        ],
        scratch_shapes=[pltpu.VMEM((HV, HK), F32)],
        compiler_params=_cparams(("parallel", "parallel")),
        name="hgrn_prompt",
    )(y, y, y, y, lb_all, g.reshape(1, HV))


def _merge_kernel(x_ref, ya_ref, yl_ref, yh_ref, ga_ref, gl_ref, gh_ref,
                  wa_ref, wl_ref, wh_ref, wo_ref, o_ref, *, exact):
    if exact:
        mm = _dot3
        cast = lambda a: a.astype(F32)
    else:
        mm = _dot
        cast = lambda a: a.astype(BF16)
    m = _sigmoid(ga_ref[...]) * mm(cast(ya_ref[...]), wa_ref[...])
    m = m + _sigmoid(gl_ref[...]) * mm(cast(yl_ref[...]), wl_ref[...])
    m = m + _sigmoid(gh_ref[...]) * mm(cast(yh_ref[...]), wh_ref[...])
    o_ref[...] = x_ref[...] + mm(cast(m), wo_ref[...])


def merge_out(x, y, ya, yl, yh, wa, wl, wh, wo, *, exact, tm):
    n = x.shape[0]
    tm = min(tm, n)
    d = D_MODEL
    row = lambda: pl.BlockSpec((tm, d), lambda i: (i, 0))
    gate = lambda c: pl.BlockSpec((tm, d), lambda i: (i, COL_GT + c))
    wspec = lambda: pl.BlockSpec((d, d), lambda i: (0, 0))
    wdt = F32 if exact else BF16
    return pl.pallas_call(
        functools.partial(_merge_kernel, exact=exact),
        grid=(n // tm,),
        in_specs=[row(), row(), row(), row(), gate(0), gate(1), gate(2), wspec(), wspec(), wspec(), wspec()],
        out_specs=row(),
        out_shape=jax.ShapeDtypeStruct((n, d), F32),
        compiler_params=_cparams(("parallel",)),
        name="merge_out",
    )(x, ya, yl, yh, y, y, y, wa.astype(wdt), wl.astype(wdt), wh.astype(wdt), wo.astype(wdt))


def _topk_rows(x, k):
    out = []
    cur = x
    for _ in range(k):
        mx = jnp.max(cur, axis=0, keepdims=True)
        out.append(mx)
        cur = jnp.where(cur == mx, NEG, cur)
    return out


def _peer_select_kernel(q_ref, kh_ref, sel_ref):
    st = _dot3_nt(kh_ref[...], q_ref[...])
    s1 = st[:N_KEYS, :]
    s2 = st[N_KEYS:, :]
    v1 = _topk_rows(s1, P_TOPK)
    v2 = _topk_rows(s2, P_TOPK)
    v1m = jnp.concatenate(v1, axis=0)
    v2m = jnp.concatenate(v2, axis=0)
    half = P_TOPK // 2
    v2_lo, v2_hi = v2m[:half], v2m[half:]

    row = lax.broadcasted_iota(jnp.int32, v2_lo.shape, 0)
    tiles = [(v1[0] + v2_lo, v2_lo), (v1[0] + v2_hi, v2_hi)]
    for a in range(1, half):
        tiles.append((jnp.where(row < P_TOPK // (a + 1), v1[a] + v2_lo, NEG), v2_lo))
    tail = v1m[half:] + v2[0]

    cur = [c for c, _ in tiles] + [tail]
    tau = None
    for _ in range(P_TOPK):
        mx = cur[0]
        for c in cur[1:]:
            mx = jnp.maximum(mx, c)
        tau = jnp.max(mx, axis=0, keepdims=True)
        cur = [jnp.where(c == tau, NEG, c) for c in cur]

    top = v1[0] + v2[0]
    big = -NEG

    def picked(cand, vals):
        sel = cand >= tau
        mass = jnp.sum(jnp.where(sel, jnp.exp(cand - top), 0.0), axis=0, keepdims=True)
        return mass, jnp.min(jnp.where(sel, vals, big), axis=0, keepdims=True)

    z0, t0 = picked(*tiles[0])
    z1, t1 = picked(*tiles[1])
    z = z0 + z1
    thetas = [jnp.minimum(t0, t1)]
    for a in range(1, half):
        za, ta = picked(*tiles[a + 1])
        z = z + za
        thetas.append(ta)
    sel_tail = tail >= tau
    z = z + jnp.sum(jnp.where(sel_tail, jnp.exp(tail - top), 0.0), axis=0, keepdims=True)
    for a in range(half, P_TOPK):
        thetas.append(jnp.where(sel_tail[a - half:a - half + 1], v2[0], big))

    theta_dense = jnp.full(s1.shape, big, F32)
    for a in range(P_TOPK):
        theta_dense = jnp.where(s1 == v1[a], thetas[a], theta_dense)
    sel_ref[0] = s2
    sel_ref[1] = jnp.exp(s2 - v2[0])
    sel_ref[2] = theta_dense
    sel_ref[3] = jnp.exp(s1 - v1[0]) / z


def peer_select(q, kbd, *, tn):
    n = q.shape[0]
    tn = min(tn, n)
    return pl.pallas_call(
        _peer_select_kernel,
        grid=(n // tn, P_HEADS),
        in_specs=[
            pl.BlockSpec((tn, D_KEY), lambda i, h: (i, h)),
            pl.BlockSpec((None, 2 * N_KEYS, D_KEY), lambda i, h: (h, 0, 0)),
        ],
        out_specs=pl.BlockSpec((None, 4, N_KEYS, tn), lambda i, h: (h, 0, 0, i)),
        out_shape=jax.ShapeDtypeStruct((P_HEADS, 4, N_KEYS, n), F32),
        compiler_params=_cparams(("parallel", "parallel")),
        name="peer_select",
    )(q, kbd)


PEER_II_GROUP = 4


def _peer_dense_kernel(xnt_ref, x_ref, sel_ref, u_ref, vt_ref, o_ref, acc_ref, w_ref, sp_ref, bc_ref, *, ec):
    e = pl.program_id(1)
    tn = w_ref.shape[1]
    n_i1 = ec // N_KEYS
    nt = tn // LANES
    tile = lambda t: slice((t % nt) * LANES, (t % nt + 1) * LANES)
    rot_sp = lambda h, k: h + 2 * k
    rot_bc = lambda ii, h, k: h + 2 * k + ii + 1

    @pl.when(e == 0)
    def _():
        acc_ref[...] = jnp.zeros_like(acc_ref)
        for h in range(P_HEADS):
            for k in range(2):
                for t in range(nt):
                    sp_ref[h, k, :, tile(t + rot_sp(h, k))] = sel_ref[h, k, :, tile(t)]

    for ii in range(n_i1):
        i1 = e * n_i1 + ii
        for h in range(P_HEADS):
            for k in range(2):
                row = sel_ref[h, 2 + k, pl.ds(i1, 1), :]
                for t in range(nt):
                    bc_ref[ii, h, k, :, tile(t + rot_bc(ii, h, k))] = jnp.broadcast_to(row[:, tile(t)],
                                                                                      (SUBLANES, LANES))

    def rows_body(r, carry):
        r0 = pl.multiple_of(r * SUBLANES, SUBLANES)
        rows = pl.ds(r0, SUBLANES)
        for g0 in range(0, n_i1, PEER_II_GROUP):
            group = range(g0, min(g0 + PEER_II_GROUP, n_i1))
            w = [{ii: None for ii in group} for _ in range(nt)]
            for h in range(P_HEADS):
                s2 = [sp_ref[h, 0, rows, tile(t + rot_sp(h, 0))] for t in range(nt)]
                p2 = [sp_ref[h, 1, rows, tile(t + rot_sp(h, 1))] for t in range(nt)]
                for ii in group:
                    for t in range(nt):
                        theta = bc_ref[ii, h, 0, :, tile(t + rot_bc(ii, h, 0))]
                        c1 = bc_ref[ii, h, 1, :, tile(t + rot_bc(ii, h, 1))]
                        term = jnp.where(s2[t] >= theta, p2[t], 0.0) * c1
                        w[t][ii] = term if w[t][ii] is None else w[t][ii] + term
            for t in range(nt):
                for ii in group:
                    w_ref[pl.ds(ii * N_KEYS + r0, SUBLANES), tile(t)] = w[t][ii]
        return carry

    lax.fori_loop(0, N_KEYS // SUBLANES, rows_body, 0)
    a_t = _dot(u_ref[...], xnt_ref[...])
    h_t = (w_ref[...] * _gelu(a_t)).astype(BF16)
    acc_ref[...] += _dot(vt_ref[...], h_t)

    @pl.when(e == pl.num_programs(1) - 1)
    def _():
        o_ref[...] = x_ref[...] + acc_ref[...].T


def peer_dense(xnt, x, sel, u, vt, *, tn, ec):
    n = x.shape[0]
    tn = min(tn, n)
    d = D_MODEL
    return pl.pallas_call(
        functools.partial(_peer_dense_kernel, ec=ec),
        grid=(n // tn, N_EXP // ec),
        in_specs=[
            pl.BlockSpec((d, tn), lambda i, e: (0, i)),
            pl.BlockSpec((tn, d), lambda i, e: (i, 0)),
            pl.BlockSpec((P_HEADS, 4, N_KEYS, tn), lambda i, e: (0, 0, 0, i)),
            pl.BlockSpec((ec, d), lambda i, e: (e, 0)),
            pl.BlockSpec((d, ec), lambda i, e: (0, e)),
        ],
        out_specs=pl.BlockSpec((tn, d), lambda i, e: (i, 0)),
        out_shape=jax.ShapeDtypeStruct((n, d), F32),
        scratch_shapes=[pltpu.VMEM((d, tn), F32), pltpu.VMEM((ec, tn), F32),
                        pltpu.VMEM((P_HEADS, 2, N_KEYS, tn), F32),
                        pltpu.VMEM((ec // N_KEYS, P_HEADS, 2, SUBLANES, tn), F32)],
        compiler_params=_cparams(("parallel", "arbitrary")),
        name="peer_dense",
    )(xnt, x, sel, u, vt)


def _rms_cast_t_kernel(x_ref, g_ref, o_ref):
    o_ref[...] = _rms(x_ref[...], g_ref[...]).T.astype(o_ref.dtype)


def rms_cast_t(x, g, *, tm):
    n, d = x.shape
    tm = min(tm, n)
    return pl.pallas_call(
        _rms_cast_t_kernel,
        grid=(n // tm,),
        in_specs=[pl.BlockSpec((tm, d), lambda i: (i, 0)), pl.BlockSpec((1, d), lambda i: (0, 0))],
        out_specs=pl.BlockSpec((d, tm), lambda i: (0, i)),
        out_shape=jax.ShapeDtypeStruct((d, n), BF16),
        compiler_params=_cparams(("parallel",)),
        name="rms_cast_t",
    )(x, g.reshape(1, d))


def _peer_key_blocks(keys):
    z = jnp.zeros_like(keys[:, 0])
    top = jnp.concatenate([keys[:, 0], z], axis=-1)
    bot = jnp.concatenate([z, keys[:, 1]], axis=-1)
    return jnp.concatenate([top, bot], axis=1)


def peer_ffn(x, g, wq, kbd, u_b, vt_b, *, tn_sel, tn, ec):
    n = x.shape[0]
    pad = (-n) % LANES
    xp = jnp.pad(x, ((0, pad), (0, 0))) if pad else x
    q = rms_matmul(xp, g, wq, exact=True, tm=512, tn=1024)
    sel = peer_select(q, kbd, tn=tn_sel)
    xnt = rms_cast_t(xp, g, tm=512)
    out = peer_dense(xnt, xp, sel, u_b, vt_b, tn=tn, ec=ec)
    return out[:n] if pad else out


def _lam_init(layer):
    return 0.8 - 0.6 * math.exp(-0.3 * layer)


def _alibi_slopes():
    return jnp.exp2(-8.0 * jnp.arange(1, N_AH + 1, dtype=F32) / N_AH)


def layer_prompt(x, l, p, w_in_b, kbd, u_b, vt_b, *, b, s):
    n = b * s
    y = rms_matmul(x, p['norm1_g'][l], w_in_b, exact=False, tm=1024, tn=1024)
    g2 = jnp.tile(p['qk_norm_g'][l], (1, 2))
    qb, kn, kb, vb, vf = qk_norm(y, g2, tm=512)
    ya = attn_prompt(qb, kb, vb, p['diff_lambda'][l], p['subln_g'][l], _alibi_slopes(),
                     b=b, s=s, lam_init=_lam_init(l), tq=1024)
    yl, conv_new, h_last = lru_prompt(y, p['conv_w'][l], p['conv_b'][l], p['rg_wa'][l], p['rg_ba'][l],
                                      p['rg_wx'][l], p['rg_bx'][l], p['rg_lambda'][l], b=b, s=s, tt=256)
    yh, st = hgrn_prompt(y, p['hgrn_lb'], p['hgrn_norm_g'][l], b=b, s=s, layer=l)
    x = merge_out(x, y, ya, yl, yh, p['w_br_attn'][l], p['w_br_lru'][l], p['w_br_hgrn'][l],
                  p['w_out'][l], exact=False, tm=512)
    x = peer_ffn(x, p['norm2_g'][l], p['peer_wq'][l], kbd, u_b, vt_b, tn_sel=512, tn=512, ec=1024)
    k_out = kn.reshape(b, s, N_AH, 2 * HD)
    v_out = vf.reshape(b, s, N_AH, VD)
    s_out = jnp.swapaxes(st, -1, -2)
    return x, (k_out, v_out, conv_new, h_last.reshape(b, LRU_W), s_out)


DEC_PPS = 8
DEC_TOK = 8


def _swap_halves(a):
    return pltpu.roll(a, HD, a.ndim - 1)


def _attn_decode_kernel(pt_ref, q_ref, kn_ref, vn_ref, slope_ref, lv_ref, sg_ref, *refs,
                        pps, past, lam_init):
    k_refs = refs[:pps]
    v_refs = refs[pps:2 * pps]
    o_ref, m_ref, l_ref, acc_ref, s_ref = refs[2 * pps:]
    step = pl.program_id(1)

    @pl.when(step == 0)
    def _():
        m_ref[...] = jnp.full(m_ref.shape, NEG, F32)
        l_ref[...] = jnp.zeros_like(l_ref)
        acc_ref[...] = jnp.zeros_like(acc_ref)

    q = q_ref[...]
    seg = _seg_matrix()
    slope = slope_ref[...]
    tbias = slope[None] * lax.broadcasted_iota(jnp.int32, (PAGE_SIZE, N_AH, LANES), 0).astype(F32)

    def rescale(m_new):
        alpha = jnp.exp(m_ref[...] - m_new)
        l_ref[...] = l_ref[...] * alpha
        acc_ref[0] = acc_ref[0] * alpha
        acc_ref[1] = acc_ref[1] * _swap_halves(alpha)
        m_ref[...] = m_new

    offs = []
    m_new = m_ref[...]
    for i in range(pps):
        prod = (k_refs[i][...] * q[None]).reshape(PAGE_SIZE * N_AH, LANES).astype(BF16)
        s = _dot(prod, seg).reshape(PAGE_SIZE, N_AH, LANES) + tbias
        s_ref[i] = s
        offs.append(slope * ((step * pps + i) * PAGE_SIZE - past).astype(F32))
        m_new = jnp.maximum(m_new, jnp.max(s, axis=0) + offs[i])
    rescale(m_new)
    shift = [m_new - off for off in offs]

    def tokens_body(g, carry):
        l_a, a_s, a_x = carry
        rows = pl.ds(pl.multiple_of(g * DEC_TOK, DEC_TOK), DEC_TOK)
        for i in range(pps):
            pw = jnp.exp(s_ref[i, rows] - shift[i])
            vals = v_refs[i][rows]
            l_a = l_a + jnp.sum(pw, axis=0)
            a_s = a_s + jnp.sum(pw * vals, axis=0)
            a_x = a_x + jnp.sum(_swap_halves(pw) * vals, axis=0)
        return l_a, a_s, a_x

    zero = jnp.zeros((N_AH, LANES), F32)
    l_a, a_s, a_x = lax.fori_loop(0, PAGE_SIZE // DEC_TOK, tokens_body, (zero, zero, zero), unroll=2)
    l_ref[...] += l_a
    acc_ref[0] += a_s
    acc_ref[1] += a_x

    @pl.when(step == pl.num_programs(1) - 1)
    def _():
        sn = _dot2_exact_rhs(kn_ref[...] * q, seg)
        rescale(jnp.maximum(m_ref[...], sn))
        pn = jnp.exp(sn - m_ref[...])
        vn = vn_ref[...]
        l_ref[...] += pn
        acc_ref[0] += pn * vn
        acc_ref[1] += _swap_halves(pn) * vn
        first = lax.broadcasted_iota(jnp.int32, (N_AH, LANES), 1) < HD
        l = l_ref[...]
        l_x = _swap_halves(l)
        a_s = acc_ref[0]
        a_x = acc_ref[1]
        o0 = jnp.where(first, a_s, a_x) / jnp.where(first, l, l_x)
        o1 = jnp.where(first, a_x, a_s) / jnp.where(first, l_x, l)
        lam = _diff_lambda(lv_ref[...], lam_init)
        o = o0 - lam * o1
        o_ref[...] = _rms(o, sg_ref[...]) * (1.0 - lam_init)


def attn_decode(page_table, q, kn, vn, cache_k, cache_v, lv, sg, slopes, *, layer, lam_init):
    bd, n_pages = page_table.shape
    pps = DEC_PPS
    past = n_pages * PAGE_SIZE
    tok = lambda: pl.BlockSpec((None, N_AH, LANES), lambda b, st, pt: (b, 0, 0))
    page = lambda i: pl.BlockSpec((None, None, PAGE_SIZE, N_AH, LANES),
                                  lambda b, st, pt: (layer, pt[b, st * pps + i], 0, 0, 0))
    grid_spec = pltpu.PrefetchScalarGridSpec(
        num_scalar_prefetch=1,
        grid=(bd, n_pages // pps),
        in_specs=[
            tok(), tok(), tok(),
            pl.BlockSpec((N_AH, LANES), lambda b, st, pt: (0, 0)),
            pl.BlockSpec((4, HD), lambda b, st, pt: (0, 0)),
            pl.BlockSpec((1, VD), lambda b, st, pt: (0, 0)),
        ] + [page(i) for i in range(pps)] + [page(i) for i in range(pps)],
        out_specs=pl.BlockSpec((None, N_AH, LANES), lambda b, st, pt: (b, 0, 0)),
        scratch_shapes=[pltpu.VMEM((N_AH, LANES), F32), pltpu.VMEM((N_AH, LANES), F32),
                        pltpu.VMEM((2, N_AH, LANES), F32),
                        pltpu.VMEM((pps, PAGE_SIZE, N_AH, LANES), F32)],
    )
    slope_tile = jnp.broadcast_to(slopes[:, None], (N_AH, LANES))
    return pl.pallas_call(
        functools.partial(_attn_decode_kernel, pps=pps, past=past, lam_init=lam_init),
        grid_spec=grid_spec,
        out_shape=jax.ShapeDtypeStruct((bd, N_AH, LANES), F32),
        compiler_params=_cparams(("parallel", "arbitrary")),
        name="attn_decode",
    )(page_table, q, kn, vn, slope_tile, lv, sg.reshape(1, VD), *([cache_k] * pps), *([cache_v] * pps))


def _lru_step_kernel(lx_ref, lg_ref, buf_ref, h0_ref, cw_ref, cb_ref, wa_ref, ba_ref, wx_ref, bx_ref,
                     lam_ref, yl_ref, conv_ref, h_ref):
    x = lx_ref[...]
    cw = cw_ref[...]
    xc = cb_ref[...] + cw[CONV_W - 1:CONV_W, :] * x
    for j in range(CONV_W - 1):
        xc = xc + cw[j:j + 1, :] * buf_ref[j]
    r, i = _lru_gates(xc, wa_ref, ba_ref[...], wx_ref, bx_ref[...], True)
    a, u = _lru_au(xc, r, i, lam_ref[...])
    h = a * h0_ref[...] + u
    yl_ref[...] = h * _gelu(lg_ref[...])
    for j in range(CONV_W - 2):
        conv_ref[j] = buf_ref[j + 1]
    conv_ref[CONV_W - 2] = x
    h_ref[...] = h


def lru_step(y, buf_t, h0, cw, cb, wa, ba, wx, bx, lam):
    bd = y.shape[0]
    full = lambda shape: pl.BlockSpec(shape, lambda i: (0,) * len(shape))
    return pl.pallas_call(
        _lru_step_kernel,
        grid=(1,),
        in_specs=[
            pl.BlockSpec((bd, LRU_W), lambda i: (0, COL_LX)),
            pl.BlockSpec((bd, LRU_W), lambda i: (0, COL_LG)),
            full((CONV_W - 1, bd, LRU_W)), full((bd, LRU_W)), full((CONV_W, LRU_W)), full((1, LRU_W)),
            full((N_LB, LB, LB)), full((1, LRU_W)), full((N_LB, LB, LB)), full((1, LRU_W)), full((1, LRU_W)),
        ],
        out_specs=[full((bd, LRU_W)), full((CONV_W - 1, bd, LRU_W)), full((bd, LRU_W))],
        out_shape=[
            jax.ShapeDtypeStruct((bd, LRU_W), F32),
            jax.ShapeDtypeStruct((CONV_W - 1, bd, LRU_W), F32),
            jax.ShapeDtypeStruct((bd, LRU_W), F32),
        ],
        compiler_params=_cparams(("arbitrary",)),
        name="lru_step",
    )(y, y, buf_t, h0, cw, cb.reshape(1, -1), wa, ba.reshape(1, -1), wx, bx.reshape(1, -1), lam.reshape(1, -1))


def _hgrn_step_kernel(hq_ref, hf_ref, hi_ref, hg_ref, lb_ref, g_ref, s0_ref, y_ref, s1_ref, *, layer):
    lb = _hgrn_lower_bound([lb_ref[d] for d in range(DEPTH)], layer)
    z = hf_ref[...]
    f = jnp.exp(_hgrn_log_f(z, lb))
    kk = (1.0 - lb) * _sigmoid(-z)
    qq = _silu(hq_ref[...])
    vv = hi_ref[...]
    pad = jnp.zeros((HK - N_HH, LANES), F32)
    cols = lambda a: jnp.concatenate([a, pad], axis=0).T
    q_t, f_t, k_t = cols(qq), cols(f), cols(kk)
    row = lax.broadcasted_iota(jnp.int32, (N_HH, LANES), 0)
    o = jnp.zeros((N_HH, LANES), F32)
    for h in range(N_HH):
        s1 = s0_ref[h] * f_t[:, h:h + 1] + k_t[:, h:h + 1] * vv[h:h + 1, :]
        s1_ref[h] = s1
        oh = jnp.sum(q_t[:, h:h + 1] * s1, axis=0, keepdims=True)
        o = jnp.where(row == h, oh, o)
    y_ref[...] = _rms(o, g_ref[...]) * _silu(hg_ref[...])


def hgrn_step(y3, lb3, g, state, *, layer):
    bd = y3.shape[0]
    col = lambda c: pl.BlockSpec((None, N_HH, LANES), lambda b: (b, c, 0))
    return pl.pallas_call(
        functools.partial(_hgrn_step_kernel, layer=layer),
        grid=(bd,),
        in_specs=[
            col(COL_HQ), col(COL_HF), col(COL_HI), col(COL_HG),
            pl.BlockSpec((DEPTH, N_HH, LANES), lambda b: (0, 0, 0)),
            pl.BlockSpec((1, HV), lambda b: (0, 0)),
            pl.BlockSpec((None, None, N_HH, HK, HV), lambda b: (layer, b, 0, 0, 0)),
        ],
        out_specs=[
            pl.BlockSpec((None, N_HH, LANES), lambda b: (b, 0, 0)),
            pl.BlockSpec((None, N_HH, HK, HV), lambda b: (b, 0, 0, 0)),
        ],
        out_shape=[
            jax.ShapeDtypeStruct((bd, N_HH, LANES), F32),
            jax.ShapeDtypeStruct((bd, N_HH, HK, HV), F32),
        ],
        compiler_params=_cparams(("parallel",)),
        name="hgrn_step",
    )(y3, y3, y3, y3, lb3, g.reshape(1, HV), state)


def layer_sample(x, l, p, cache_k, cache_v, page_table, state_conv, state_lru, state_hgrn, kbd, u_b, vt_b):
    bd = x.shape[0]
    y = rms_matmul(x, p['norm1_g'][l], p['w_in'][l], exact=True, tm=bd, tn=1024)
    g2 = jnp.tile(p['qk_norm_g'][l], (1, 2))
    qf, kn, _, _, v_new = qk_norm(y, g2, tm=bd, q_dtype=F32)
    tok = lambda a: a.reshape(bd, N_AH, LANES)
    ya = attn_decode(page_table, tok(qf), tok(kn), tok(v_new), cache_k, cache_v, p['diff_lambda'][l],
                     p['subln_g'][l], _alibi_slopes(), layer=l, lam_init=_lam_init(l))
    yl, conv_t, h_new = lru_step(y, jnp.swapaxes(state_conv[l], 0, 1), state_lru[l], p['conv_w'][l],
                                 p['conv_b'][l], p['rg_wa'][l], p['rg_ba'][l], p['rg_wx'][l], p['rg_bx'][l],
                                 p['rg_lambda'][l])
    yh, s1 = hgrn_step(y.reshape(bd, IN_W // LANES, LANES), p['hgrn_lb'].reshape(DEPTH, N_HH, LANES),
                       p['hgrn_norm_g'][l], state_hgrn, layer=l)
    x = merge_out(x, y, ya.reshape(bd, ATTN_W), yl, yh.reshape(bd, HGRN_W), p['w_br_attn'][l],
                  p['w_br_lru'][l], p['w_br_hgrn'][l], p['w_out'][l], exact=True, tm=bd)
    x = peer_ffn(x, p['norm2_g'][l], p['peer_wq'][l], kbd, u_b, vt_b, tn_sel=512, tn=512, ec=1024)
    return x, (kn.reshape(bd, 1, N_AH, 2 * HD), v_new.reshape(bd, 1, N_AH, VD), jnp.swapaxes(conv_t, 0, 1),
               h_new, s1)


def kernel(x_prompt, x_sample, cache_k, cache_v, state_conv, state_lru, state_hgrn, page_table,
           norm1_g, norm2_g, w_in, qk_norm_g, diff_lambda, subln_g, conv_w, conv_b,
           rg_wa, rg_ba, rg_wx, rg_bx, rg_lambda, hgrn_lb, hgrn_norm_g,
           w_br_attn, w_br_lru, w_br_hgrn, w_out, peer_wq, peer_keys, peer_u, peer_v):
    p = dict(norm1_g=norm1_g, norm2_g=norm2_g, w_in=w_in, qk_norm_g=qk_norm_g, diff_lambda=diff_lambda,
             subln_g=subln_g, conv_w=conv_w, conv_b=conv_b, rg_wa=rg_wa, rg_ba=rg_ba, rg_wx=rg_wx,
             rg_bx=rg_bx, rg_lambda=rg_lambda, hgrn_lb=hgrn_lb, hgrn_norm_g=hgrn_norm_g,
             w_br_attn=w_br_attn, w_br_lru=w_br_lru, w_br_hgrn=w_br_hgrn, w_out=w_out,
             peer_wq=peer_wq, peer_keys=peer_keys, peer_u=peer_u, peer_v=peer_v)
    b, s, _ = x_prompt.shape
    bd = x_sample.shape[0]
    xp = x_prompt.reshape(b * s, D_MODEL)
    xs = x_sample.reshape(bd, D_MODEL)
    st_p, st_s = [], []
    for l in range(DEPTH):
        w_in_b = w_in[l].astype(BF16)
        kbd = _peer_key_blocks(peer_keys[l])
        u_b = peer_u[l].astype(BF16)
        vt_b = peer_v[l].T.astype(BF16)
        xp, sp = layer_prompt(xp, l, p, w_in_b, kbd, u_b, vt_b, b=b, s=s)
        xs, ss = layer_sample(xs, l, p, cache_k, cache_v, page_table, state_conv, state_lru, state_hgrn,
                              kbd, u_b, vt_b)
        st_p.append(sp)
        st_s.append(ss)
    k_p, v_p, conv_p, lru_p, hgrn_p = [jnp.stack(a) for a in zip(*st_p)]
    k_s, v_s, conv_s, lru_s, hgrn_s = [jnp.stack(a) for a in zip(*st_s)]
    return (xp.reshape(b, s, D_MODEL), xs.reshape(bd, 1, D_MODEL), k_p, v_p, conv_p, lru_p, hgrn_p,
            k_s, v_s, conv_s, lru_s, hgrn_s)
```

```python
import functools
import math

import jax
import jax.numpy as jnp
from jax import lax
from jax.experimental import pallas as pl
from jax.experimental.pallas import tpu as pltpu

F32 = jnp.float32
BF16 = jnp.bfloat16

D_MODEL = 1024
DEPTH = 2
PAGE_SIZE = 128
N_AH = 8
HD = 64
VD = 2 * HD
ATTN_W = N_AH * VD
LRU_W = 1024
N_LB = 8
LB = LRU_W // N_LB
CONV_W = 4
LRU_C = 8.0
N_HH = 8
HK = 128
HV = 128
HGRN_W = N_HH * HV
P_HEADS = 8
N_KEYS = 128
N_EXP = N_KEYS * N_KEYS
D_KEY = 128
P_TOPK = 16
EPS = 1e-6
IN_W = 12 * D_MODEL

COL_Q, COL_K, COL_V, COL_LX, COL_LG, COL_HQ, COL_HF, COL_HI, COL_HG, COL_GT = 0, 1, 2, 3, 4, 5, 6, 7, 8, 9

LANES = 128
SUBLANES = 8
VMEM_LIMIT = 56 * 1024 * 1024

NEG = -1e30
SQRT_HALF = 0.7071067811865476


def _cparams(sem, flags=None):
    return pltpu.CompilerParams(dimension_semantics=sem, vmem_limit_bytes=VMEM_LIMIT, flags=flags)


def _split(a):
    hi = a.astype(BF16)
    lo = (a - hi.astype(F32)).astype(BF16)
    return hi, lo


def _dot(a, b):
    return jnp.dot(a, b, preferred_element_type=F32)


def _dot_nt(a, b):
    return lax.dot_general(a, b, (((1,), (1,)), ((), ())), preferred_element_type=F32)


def _dot3(a, b):
    ah, al = _split(a)
    bh, bl = _split(b)
    return _dot(ah, bh) + _dot(al, bh) + _dot(ah, bl)


def _dot3_nt(a, b):
    ah, al = _split(a)
    bh, bl = _split(b)
    return _dot_nt(ah, bh) + _dot_nt(al, bh) + _dot_nt(ah, bl)


def _dot2_exact_rhs(a, b_bf16):
    ah, al = _split(a)
    return _dot(ah, b_bf16) + _dot(al, b_bf16)


def _sigmoid(x):
    return 1.0 / (1.0 + jnp.exp(-x))


def _gelu(x):
    return 0.5 * x * (1.0 + lax.erf(x * SQRT_HALF))


def _silu(x):
    return x * _sigmoid(x)


def _softplus(x):
    return jnp.maximum(x, 0.0) + jnp.log1p(jnp.exp(-jnp.abs(x)))


def _expm1(x):
    u = jnp.exp(x)
    um1 = u - 1.0
    corrected = um1 * x / jnp.log(u)
    return jnp.where(um1 == 0.0, x, jnp.where(um1 == -1.0, -1.0, corrected))


def _rms(x, g):
    ms = jnp.mean(x * x, axis=-1, keepdims=True)
    return x * lax.rsqrt(ms + EPS) * g


def _rms_mm_kernel(x_ref, g_ref, w_ref, o_ref, xn_ref, *, exact):
    @pl.when(pl.program_id(1) == 0)
    def _():
        xn_ref[...] = _rms(x_ref[...], g_ref[...]).astype(xn_ref.dtype)

    if exact:
        o_ref[...] = _dot3(xn_ref[...], w_ref[...])
    else:
        o_ref[...] = _dot(xn_ref[...], w_ref[...])


def rms_matmul(x, g, w, *, exact, tm, tn):
    n, d = x.shape
    wd = w.shape[1]
    tm = min(tm, n)
    tn = min(tn, wd)
    return pl.pallas_call(
        functools.partial(_rms_mm_kernel, exact=exact),
        grid=(n // tm, wd // tn),
        in_specs=[
            pl.BlockSpec((tm, d), lambda i, j: (i, 0)),
            pl.BlockSpec((1, d), lambda i, j: (0, 0)),
            pl.BlockSpec((d, tn), lambda i, j: (0, j)),
        ],
        out_specs=pl.BlockSpec((tm, tn), lambda i, j: (i, j)),
        out_shape=jax.ShapeDtypeStruct((n, wd), F32),
        scratch_shapes=[pltpu.VMEM((tm, d), F32 if exact else BF16)],
        compiler_params=_cparams(("parallel", "arbitrary")),
        name="rms_matmul",
    )(x, g.reshape(1, d), w)


def _seg_matrix():
    r = lax.broadcasted_iota(jnp.int32, (LANES, LANES), 0) // HD
    c = lax.broadcasted_iota(jnp.int32, (LANES, LANES), 1) // HD
    return jnp.where(r == c, 1.0, 0.0).astype(BF16)


def _subhead_norm(x, g, seg):
    ss = _dot2_exact_rhs(x * x, seg)
    return x * lax.rsqrt(ss * (1.0 / HD) + EPS) * g


def _qk_norm_kernel(q_ref, k_ref, v_ref, g_ref, qb_ref, kn_ref, kb_ref, vb_ref, vf_ref):
    seg = _seg_matrix()
    gq = g_ref[0:1, :]
    gk = g_ref[1:2, :]
    for h in range(N_AH):
        sl = slice(h * LANES, (h + 1) * LANES)
        qn = _subhead_norm(q_ref[:, sl], gq, seg)
        kn = _subhead_norm(k_ref[:, sl], gk, seg)
        qb_ref[:, sl] = (qn * (HD ** -0.5)).astype(qb_ref.dtype)
        kn_ref[:, sl] = kn
        kb_ref[:, sl] = kn.astype(BF16)
    v = v_ref[...]
    vb_ref[...] = v.astype(BF16)
    vf_ref[...] = v


def qk_norm(y, g2, *, tm, q_dtype=BF16):
    n = y.shape[0]
    tm = min(tm, n)
    w = ATTN_W
    col = lambda c: pl.BlockSpec((tm, w), lambda i: (i, c))
    out = pl.BlockSpec((tm, w), lambda i: (i, 0))
    return pl.pallas_call(
        _qk_norm_kernel,
        grid=(n // tm,),
        in_specs=[col(COL_Q), col(COL_K), col(COL_V), pl.BlockSpec((2, LANES), lambda i: (0, 0))],
        out_specs=[out, out, out, out, out],
        out_shape=[
            jax.ShapeDtypeStruct((n, w), q_dtype),
            jax.ShapeDtypeStruct((n, w), F32),
            jax.ShapeDtypeStruct((n, w), BF16),
            jax.ShapeDtypeStruct((n, w), BF16),
            jax.ShapeDtypeStruct((n, w), F32),
        ],
        compiler_params=_cparams(("parallel",)),
        name="qk_norm",
    )(y, y, y, g2)


def _diff_lambda(lv, lam_init):
    t1 = jnp.sum(lv[0:1, :] * lv[1:2, :], axis=-1, keepdims=True)
    t2 = jnp.sum(lv[2:3, :] * lv[3:4, :], axis=-1, keepdims=True)
    return jnp.exp(t1) - jnp.exp(t2) + lam_init


def _attn_kernel(slope_ref, lv_ref, sg_ref, q_ref, k_ref, v_ref, o_ref, m_ref, acc_ref, *, tq, lam_init):
    h = pl.program_id(1)
    qi = pl.program_id(2)
    slope = slope_ref[h]
    q = q_ref[...]
    first = lax.broadcasted_iota(jnp.int32, (tq, LANES), 1) < HD
    col = lax.broadcasted_iota(jnp.int32, (1, tq), 1)
    ones = jnp.ones((tq, LANES), BF16)
    m_ref[...] = jnp.full(m_ref.shape, NEG, F32)
    acc_ref[...] = jnp.zeros_like(acc_ref)

    def update(c, s, v1):
        m_old = m_ref[c]
        m_new = jnp.maximum(m_old, jnp.max(s, axis=-1, keepdims=True))
        p = jnp.exp(s - m_new).astype(BF16)
        acc_ref[c] = acc_ref[c] * jnp.exp(m_old - m_new) + _dot(p, v1)
        m_ref[c] = m_new

    def block(j, masked):
        start = pl.multiple_of(j * tq, tq)
        kk = k_ref[pl.ds(start, tq), :]
        v1 = jnp.concatenate([v_ref[pl.ds(start, tq), :], ones], axis=-1)
        zero = jnp.zeros_like(kk)
        bias = slope * ((j - qi) * tq + col).astype(F32)
        for c in range(2):
            kc = jnp.where(first, kk, zero) if c == 0 else jnp.where(first, zero, kk)
            s = _dot_nt(q, kc) + bias
            if masked:
                row2 = lax.broadcasted_iota(jnp.int32, (tq, tq), 0)
                col2 = lax.broadcasted_iota(jnp.int32, (tq, tq), 1)
                s = jnp.where(col2 <= row2, s, NEG)
            update(c, s, v1)

    def body(j, carry):
        block(j, False)
        return carry

    lax.fori_loop(0, qi, body, 0)
    block(qi, True)

    lam = _diff_lambda(lv_ref[...], lam_init)
    r0 = acc_ref[0]
    r1 = acc_ref[1]
    o = r0[:, :LANES] / r0[:, LANES:] - lam * (r1[:, :LANES] / r1[:, LANES:])
    o_ref[...] = _rms(o, sg_ref[...]) * (1.0 - lam_init)


def attn_prompt(qb, kb, vb, lv, sg, slopes, *, b, s, lam_init, tq):
    n = b * s
    tq = min(tq, s)
    nq = s // tq
    return pl.pallas_call(
        functools.partial(_attn_kernel, tq=tq, lam_init=lam_init),
        grid=(b, N_AH, nq),
        in_specs=[
            pl.BlockSpec(memory_space=pltpu.SMEM),
            pl.BlockSpec((4, HD), lambda bi, h, i: (0, 0)),
            pl.BlockSpec((1, VD), lambda bi, h, i: (0, 0)),
            pl.BlockSpec((tq, LANES), lambda bi, h, i: (bi * nq + i, h)),
            pl.BlockSpec((s, LANES), lambda bi, h, i: (bi, h)),
            pl.BlockSpec((s, LANES), lambda bi, h, i: (bi, h)),
        ],
        out_specs=pl.BlockSpec((tq, LANES), lambda bi, h, i: (bi * nq + i, h)),
        out_shape=jax.ShapeDtypeStruct((n, ATTN_W), F32),
        scratch_shapes=[pltpu.VMEM((2, tq, 1), F32), pltpu.VMEM((2, tq, 2 * LANES), F32)],
        compiler_params=_cparams(("parallel", "parallel", "arbitrary")),
        name="attn_prompt",
    )(slopes, lv, sg.reshape(1, VD), qb, kb, vb)


def _scan_rows(a, u):
    tt = a.shape[0]
    row = lax.broadcasted_iota(jnp.int32, a.shape, 0)
    s = 1
    while s < tt:
        keep = row >= s
        a_prev = jnp.where(keep, pltpu.roll(a, s, 0), 1.0)
        u_prev = jnp.where(keep, pltpu.roll(u, s, 0), 0.0)
        u = a * u_prev + u
        a = a * a_prev
        s *= 2
    return a, u


def _lru_gates(xc, wa_ref, ba, wx_ref, bx, exact):
    rs, is_ = [], []
    for nb in range(N_LB):
        xs = xc[:, nb * LB:(nb + 1) * LB]
        if exact:
            rs.append(_dot3(xs, wa_ref[nb]))
            is_.append(_dot3(xs, wx_ref[nb]))
        else:
            xb = xs.astype(BF16)
            rs.append(_dot(xb, wa_ref[nb]))
            is_.append(_dot(xb, wx_ref[nb]))
    r = _sigmoid(jnp.concatenate(rs, axis=-1) + ba)
    i = _sigmoid(jnp.concatenate(is_, axis=-1) + bx)
    return r, i


def _lru_au(xc, r, i, lam):
    log_a = -LRU_C * r * _softplus(-lam)
    a = jnp.exp(log_a)
    u = jnp.sqrt(-_expm1(2.0 * log_a)) * (i * xc)
    return a, u


def _lru_kernel(lx_ref, lg_ref, cw_ref, cb_ref, wa_ref, ba_ref, wx_ref, bx_ref, lam_ref,
                y_ref, conv_ref, hl_ref, xbuf, hc, *, tt):
    t = pl.program_id(1)
    pad = SUBLANES

    @pl.when(t == 0)
    def _():
        xbuf[0:pad, :] = jnp.zeros((pad, LRU_W), F32)
        hc[...] = jnp.zeros_like(hc)

    x = lx_ref[...]
    xbuf[pad:pad + tt, :] = x
    cw = cw_ref[...]
    xc = cb_ref[...] + cw[3:4, :] * x
    for j in range(CONV_W - 1):
        back = CONV_W - 1 - j
        xc = xc + cw[j:j + 1, :] * xbuf[pad - back:pad - back + tt, :]
    xbuf[0:pad, :] = x[tt - pad:tt, :]

    r, i = _lru_gates(xc, wa_ref, ba_ref[...], wx_ref, bx_ref[...], False)
    a, u = _lru_au(xc, r, i, lam_ref[...])
    ca, cu = _scan_rows(a, u)
    hseq = ca * hc[0:1, :] + cu
    hc[0:1, :] = hseq[tt - 1:tt, :]
    y_ref[...] = (hseq * _gelu(lg_ref[...])).astype(y_ref.dtype)

    @pl.when(t == pl.num_programs(1) - 1)
    def _():
        conv_ref[...] = x[tt - (CONV_W - 1):tt, :]
        hl_ref[...] = hseq[tt - 1:tt, :]


def lru_prompt(y, cw, cb, wa, ba, wx, bx, lam, *, b, s, tt):
    n = b * s
    tt = min(tt, s)
    nt = s // tt
    vec = lambda: pl.BlockSpec((1, LRU_W), lambda bi, t: (0, 0))
    mat = lambda: pl.BlockSpec((N_LB, LB, LB), lambda bi, t: (0, 0, 0))
    return pl.pallas_call(
        functools.partial(_lru_kernel, tt=tt),
        grid=(b, nt),
        in_specs=[
            pl.BlockSpec((tt, LRU_W), lambda bi, t: (bi * nt + t, COL_LX)),
            pl.BlockSpec((tt, LRU_W), lambda bi, t: (bi * nt + t, COL_LG)),
            pl.BlockSpec((CONV_W, LRU_W), lambda bi, t: (0, 0)),
            vec(), mat(), vec(), mat(), vec(), vec(),
        ],
        out_specs=[
            pl.BlockSpec((tt, LRU_W), lambda bi, t: (bi * nt + t, 0)),
            pl.BlockSpec((None, CONV_W - 1, LRU_W), lambda bi, t: (bi, 0, 0)),
            pl.BlockSpec((None, 1, LRU_W), lambda bi, t: (bi, 0, 0)),
        ],
        out_shape=[
            jax.ShapeDtypeStruct((n, LRU_W), BF16),
            jax.ShapeDtypeStruct((b, CONV_W - 1, LRU_W), F32),
            jax.ShapeDtypeStruct((b, 1, LRU_W), F32),
        ],
        scratch_shapes=[pltpu.VMEM((tt + SUBLANES, LRU_W), F32), pltpu.VMEM((SUBLANES, LRU_W), F32)],
        compiler_params=_cparams(("parallel", "arbitrary")),
        name="lru_prompt",
    )(y, y, cw, cb.reshape(1, -1), wa.astype(BF16), ba.reshape(1, -1), wx.astype(BF16),
      bx.reshape(1, -1), lam.reshape(1, -1))


H_CHUNK = 64
H_SUB = SUBLANES


def _hgrn_lower_bound(rows, layer):
    mx = rows[0]
    for r in rows[1:]:
        mx = jnp.maximum(mx, r)
    es = [jnp.exp(r - mx) for r in rows]
    tot = es[0]
    for e in es[1:]:
        tot = tot + e
    lb = jnp.zeros_like(mx)
    for d in range(1, layer + 1):
        lb = lb + es[d] / tot
    return lb


def _hgrn_log_f(z, lb):
    log_sig = jnp.minimum(z, 0.0) - jnp.log1p(jnp.exp(-jnp.abs(z)))
    a = jnp.log(lb)
    bb = jnp.log1p(-lb) + log_sig
    return jnp.maximum(a, bb) + jnp.log1p(jnp.exp(-jnp.abs(a - bb)))


def _cumsum_rows(x):
    n = x.shape[0]
    row = lax.broadcasted_iota(jnp.int32, x.shape, 0)
    s = 1
    while s < n:
        x = x + jnp.where(row >= s, pltpu.roll(x, s, 0), 0.0)
        s *= 2
    return x


def _hgrn_chunk(qq, kk, logf, vv, st):
    c = qq.shape[0]
    nsub = c // H_SUB
    bcum = _cumsum_rows(logf)
    row_c = lax.broadcasted_iota(jnp.int32, (c, LANES), 0)
    row_s = lax.broadcasted_iota(jnp.int32, (H_SUB, LANES), 0)
    vb = vv.astype(BF16)

    o = _dot_nt((qq * jnp.exp(bcum)).astype(BF16), st.astype(BF16))

    att_rows = []
    diag_rows = []
    for i in range(nsub):
        r0 = i * H_SUB
        q_r = qq[r0:r0 + H_SUB, :]
        b_r = bcum[r0:r0 + H_SUB, :]
        if i == 0:
            att_rows.append(jnp.zeros((H_SUB, c), F32))
        else:
            beta = bcum[r0 - 1:r0, :]
            qt = q_r * jnp.exp(b_r - beta)
            kt = kk[:r0, :] * jnp.exp(beta - bcum[:r0, :])
            kt = jnp.concatenate([kt, jnp.zeros((c - r0, LANES), F32)], axis=0)
            att_rows.append(_dot_nt(qt.astype(BF16), kt.astype(BF16)))
        k_r = kk[r0:r0 + H_SUB, :]
        v_r = vv[r0:r0 + H_SUB, :]
        od = jnp.zeros((H_SUB, LANES), F32)
        for s in range(H_SUB):
            e = jnp.exp(jnp.where(row_s >= s, b_r - b_r[s:s + 1, :], NEG))
            w = jnp.sum(q_r * k_r[s:s + 1, :] * e, axis=-1, keepdims=True)
            od = od + w * v_r[s:s + 1, :]
        diag_rows.append(od)
    att = jnp.concatenate(att_rows, axis=0)
    o = o + _dot(att.astype(BF16), vb) + jnp.concatenate(diag_rows, axis=0)

    b_last = bcum[c - 1:c, :]
    kd = (kk * jnp.exp(b_last - bcum)).astype(BF16)
    st = st * jnp.exp(b_last) + lax.dot_general(vb, kd, (((0,), (0,)), ((), ())),
                                                 preferred_element_type=F32)
    return o, st


def _hgrn_kernel(hq_ref, hf_ref, hi_ref, hg_ref, lb_ref, g_ref, y_ref, s_ref, st_ref, *, layer, nchunk):
    lb = _hgrn_lower_bound([lb_ref[d:d + 1, :] for d in range(DEPTH)], layer)
    g = g_ref[...]
    st_ref[...] = jnp.zeros_like(st_ref)

    def body(ci, carry):
        r0 = pl.multiple_of(ci * H_CHUNK, H_CHUNK)
        rows = pl.ds(r0, H_CHUNK)
        z = hf_ref[rows, :]
        logf = _hgrn_log_f(z, lb)
        kk = (1.0 - lb) * _sigmoid(-z)
        qq = _silu(hq_ref[rows, :])
        o, st = _hgrn_chunk(qq, kk, logf, hi_ref[rows, :], st_ref[...])
        st_ref[...] = st
        y_ref[rows, :] = (_rms(o, g) * _silu(hg_ref[rows, :])).astype(y_ref.dtype)
        return carry

    lax.fori_loop(0, nchunk, body, 0, unroll=4)
    s_ref[...] = st_ref[...]


def hgrn_prompt(y, lb_all, g, *, b, s, layer):
    n = b * s
    nchunk = s // H_CHUNK
    col = lambda c: pl.BlockSpec((s, LANES), lambda bi, h: (bi, c * N_HH + h))
    return pl.pallas_call(
        functools.partial(_hgrn_kernel, layer=layer, nchunk=nchunk),
        grid=(b, N_HH),
        in_specs=[
            col(COL_HQ), col(COL_HF), col(COL_HI), col(COL_HG),
            pl.BlockSpec((DEPTH, LANES), lambda bi, h: (0, h)),
            pl.BlockSpec((1, HV), lambda bi, h: (0, 0)),
        ],
        out_specs=[
            pl.BlockSpec((s, LANES), lambda bi, h: (bi, h)),
            pl.BlockSpec((None, None, HV, HK), lambda bi, h: (bi, h, 0, 0)),
        ],
        out_shape=[
            jax.ShapeDtypeStruct((n, HGRN_W), BF16),
            jax.ShapeDtypeStruct((b, N_HH, HV, HK), F32),
        ],
        scratch_shapes=[pltpu.VMEM((HV, HK), F32)],
        compiler_params=_cparams(("parallel", "parallel")),
        name="hgrn_prompt",
    )(y, y, y, y, lb_all, g.reshape(1, HV))


def _merge_kernel(x_ref, ya_ref, yl_ref, yh_ref, ga_ref, gl_ref, gh_ref,
                  wa_ref, wl_ref, wh_ref, wo_ref, o_ref, *, exact):
    if exact:
        mm = _dot3
        cast = lambda a: a.astype(F32)
    else:
        mm = _dot
        cast = lambda a: a.astype(BF16)
    m = _sigmoid(ga_ref[...]) * mm(cast(ya_ref[...]), wa_ref[...])
    m = m + _sigmoid(gl_ref[...]) * mm(cast(yl_ref[...]), wl_ref[...])
    m = m + _sigmoid(gh_ref[...]) * mm(cast(yh_ref[...]), wh_ref[...])
    o_ref[...] = x_ref[...] + mm(cast(m), wo_ref[...])


def merge_out(x, y, ya, yl, yh, wa, wl, wh, wo, *, exact, tm):
    n = x.shape[0]
    tm = min(tm, n)
    d = D_MODEL
    row = lambda: pl.BlockSpec((tm, d), lambda i: (i, 0))
    gate = lambda c: pl.BlockSpec((tm, d), lambda i: (i, COL_GT + c))
    wspec = lambda: pl.BlockSpec((d, d), lambda i: (0, 0))
    wdt = F32 if exact else BF16
    return pl.pallas_call(
        functools.partial(_merge_kernel, exact=exact),
        grid=(n // tm,),
        in_specs=[row(), row(), row(), row(), gate(0), gate(1), gate(2), wspec(), wspec(), wspec(), wspec()],
        out_specs=row(),
        out_shape=jax.ShapeDtypeStruct((n, d), F32),
        compiler_params=_cparams(("parallel",)),
        name="merge_out",
    )(x, ya, yl, yh, y, y, y, wa.astype(wdt), wl.astype(wdt), wh.astype(wdt), wo.astype(wdt))


def _topk_rows(x, k):
    out = []
    cur = x
    for _ in range(k):
        mx = jnp.max(cur, axis=0, keepdims=True)
        out.append(mx)
        cur = jnp.where(cur == mx, NEG, cur)
    return out


def _peer_select_kernel(q_ref, kh_ref, sel_ref):
    st = _dot3_nt(kh_ref[...], q_ref[...])
    s1 = st[:N_KEYS, :]
    s2 = st[N_KEYS:, :]
    v1 = _topk_rows(s1, P_TOPK)
    v2 = _topk_rows(s2, P_TOPK)
    v1m = jnp.concatenate(v1, axis=0)
    v2m = jnp.concatenate(v2, axis=0)
    half = P_TOPK // 2
    v2_lo, v2_hi = v2m[:half], v2m[half:]

    row = lax.broadcasted_iota(jnp.int32, v2_lo.shape, 0)
    tiles = [(v1[0] + v2_lo, v2_lo), (v1[0] + v2_hi, v2_hi)]
    for a in range(1, half):
        tiles.append((jnp.where(row < P_TOPK // (a + 1), v1[a] + v2_lo, NEG), v2_lo))
    tail = v1m[half:] + v2[0]

    cur = [c for c, _ in tiles] + [tail]
    tau = None
    for _ in range(P_TOPK):
        mx = cur[0]
        for c in cur[1:]:
            mx = jnp.maximum(mx, c)
        tau = jnp.max(mx, axis=0, keepdims=True)
        cur = [jnp.where(c == tau, NEG, c) for c in cur]

    top = v1[0] + v2[0]
    big = -NEG

    def picked(cand, vals):
        sel = cand >= tau
        mass = jnp.sum(jnp.where(sel, jnp.exp(cand - top), 0.0), axis=0, keepdims=True)
        return mass, jnp.min(jnp.where(sel, vals, big), axis=0, keepdims=True)

    z0, t0 = picked(*tiles[0])
    z1, t1 = picked(*tiles[1])
    z = z0 + z1
    thetas = [jnp.minimum(t0, t1)]
    for a in range(1, half):
        za, ta = picked(*tiles[a + 1])
        z = z + za
        thetas.append(ta)
    sel_tail = tail >= tau
    z = z + jnp.sum(jnp.where(sel_tail, jnp.exp(tail - top), 0.0), axis=0, keepdims=True)
    for a in range(half, P_TOPK):
        thetas.append(jnp.where(sel_tail[a - half:a - half + 1], v2[0], big))

    theta_dense = jnp.full(s1.shape, big, F32)
    for a in range(P_TOPK):
        theta_dense = jnp.where(s1 == v1[a], thetas[a], theta_dense)
    sel_ref[0] = s2
    sel_ref[1] = jnp.exp(s2 - v2[0])
    sel_ref[2] = theta_dense
    sel_ref[3] = jnp.exp(s1 - v1[0]) / z


def peer_select(q, kbd, *, tn):
    n = q.shape[0]
    tn = min(tn, n)
    return pl.pallas_call(
        _peer_select_kernel,
        grid=(n // tn, P_HEADS),
        in_specs=[
            pl.BlockSpec((tn, D_KEY), lambda i, h: (i, h)),
            pl.BlockSpec((None, 2 * N_KEYS, D_KEY), lambda i, h: (h, 0, 0)),
        ],
        out_specs=pl.BlockSpec((None, 4, N_KEYS, tn), lambda i, h: (h, 0, 0, i)),
        out_shape=jax.ShapeDtypeStruct((P_HEADS, 4, N_KEYS, n), F32),
        compiler_params=_cparams(("parallel", "parallel")),
        name="peer_select",
    )(q, kbd)


PEER_II_GROUP = 4


def _peer_dense_kernel(xnt_ref, x_ref, sel_ref, u_ref, vt_ref, o_ref, acc_ref, w_ref, sp_ref, bc_ref, *, ec):
    e = pl.program_id(1)
    tn = w_ref.shape[1]
    n_i1 = ec // N_KEYS
    nt = tn // LANES
    tile = lambda t: slice((t % nt) * LANES, (t % nt + 1) * LANES)
    rot_sp = lambda h, k: h + 2 * k
    rot_bc = lambda ii, h, k: h + 2 * k + ii + 1

    @pl.when(e == 0)
    def _():
        acc_ref[...] = jnp.zeros_like(acc_ref)
        for h in range(P_HEADS):
            for k in range(2):
                for t in range(nt):
                    sp_ref[h, k, :, tile(t + rot_sp(h, k))] = sel_ref[h, k, :, tile(t)]

    for ii in range(n_i1):
        i1 = e * n_i1 + ii
        for h in range(P_HEADS):
            for k in range(2):
                row = sel_ref[h, 2 + k, pl.ds(i1, 1), :]
                for t in range(nt):
                    bc_ref[ii, h, k, :, tile(t + rot_bc(ii, h, k))] = jnp.broadcast_to(row[:, tile(t)],
                                                                                      (SUBLANES, LANES))

    def rows_body(r, carry):
        r0 = pl.multiple_of(r * SUBLANES, SUBLANES)
        rows = pl.ds(r0, SUBLANES)
        for g0 in range(0, n_i1, PEER_II_GROUP):
            group = range(g0, min(g0 + PEER_II_GROUP, n_i1))
            w = [{ii: None for ii in group} for _ in range(nt)]
            for h in range(P_HEADS):
                s2 = [sp_ref[h, 0, rows, tile(t + rot_sp(h, 0))] for t in range(nt)]
                p2 = [sp_ref[h, 1, rows, tile(t + rot_sp(h, 1))] for t in range(nt)]
                for ii in group:
                    for t in range(nt):
                        theta = bc_ref[ii, h, 0, :, tile(t + rot_bc(ii, h, 0))]
                        c1 = bc_ref[ii, h, 1, :, tile(t + rot_bc(ii, h, 1))]
                        term = jnp.where(s2[t] >= theta, p2[t], 0.0) * c1
                        w[t][ii] = term if w[t][ii] is None else w[t][ii] + term
            for t in range(nt):
                for ii in group:
                    w_ref[pl.ds(ii * N_KEYS + r0, SUBLANES), tile(t)] = w[t][ii]
        return carry

    lax.fori_loop(0, N_KEYS // SUBLANES, rows_body, 0)
    a_t = _dot(u_ref[...], xnt_ref[...])
    h_t = (w_ref[...] * _gelu(a_t)).astype(BF16)
    acc_ref[...] += _dot(vt_ref[...], h_t)

    @pl.when(e == pl.num_programs(1) - 1)
    def _():
        o_ref[...] = x_ref[...] + acc_ref[...].T


def peer_dense(xnt, x, sel, u, vt, *, tn, ec):
    n = x.shape[0]
    tn = min(tn, n)
    d = D_MODEL
    return pl.pallas_call(
        functools.partial(_peer_dense_kernel, ec=ec),
        grid=(n // tn, N_EXP // ec),
        in_specs=[
            pl.BlockSpec((d, tn), lambda i, e: (0, i)),
            pl.BlockSpec((tn, d), lambda i, e: (i, 0)),
            pl.BlockSpec((P_HEADS, 4, N_KEYS, tn), lambda i, e: (0, 0, 0, i)),
            pl.BlockSpec((ec, d), lambda i, e: (e, 0)),
            pl.BlockSpec((d, ec), lambda i, e: (0, e)),
        ],
        out_specs=pl.BlockSpec((tn, d), lambda i, e: (i, 0)),
        out_shape=jax.ShapeDtypeStruct((n, d), F32),
        scratch_shapes=[pltpu.VMEM((d, tn), F32), pltpu.VMEM((ec, tn), F32),
                        pltpu.VMEM((P_HEADS, 2, N_KEYS, tn), F32),
                        pltpu.VMEM((ec // N_KEYS, P_HEADS, 2, SUBLANES, tn), F32)],
        compiler_params=_cparams(("parallel", "arbitrary")),
        name="peer_dense",
    )(xnt, x, sel, u, vt)


def _rms_cast_t_kernel(x_ref, g_ref, o_ref):
    o_ref[...] = _rms(x_ref[...], g_ref[...]).T.astype(o_ref.dtype)


def rms_cast_t(x, g, *, tm):
    n, d = x.shape
    tm = min(tm, n)
    return pl.pallas_call(
        _rms_cast_t_kernel,
        grid=(n // tm,),
        in_specs=[pl.BlockSpec((tm, d), lambda i: (i, 0)), pl.BlockSpec((1, d), lambda i: (0, 0))],
        out_specs=pl.BlockSpec((d, tm), lambda i: (0, i)),
        out_shape=jax.ShapeDtypeStruct((d, n), BF16),
        compiler_params=_cparams(("parallel",)),
        name="rms_cast_t",
    )(x, g.reshape(1, d))


def _peer_key_blocks(keys):
    z = jnp.zeros_like(keys[:, 0])
    top = jnp.concatenate([keys[:, 0], z], axis=-1)
    bot = jnp.concatenate([z, keys[:, 1]], axis=-1)
    return jnp.concatenate([top, bot], axis=1)


def peer_ffn(x, g, wq, kbd, u_b, vt_b, *, tn_sel, tn, ec):
    n = x.shape[0]
    pad = (-n) % LANES
    xp = jnp.pad(x, ((0, pad), (0, 0))) if pad else x
    q = rms_matmul(xp, g, wq, exact=True, tm=512, tn=1024)
    sel = peer_select(q, kbd, tn=tn_sel)
    xnt = rms_cast_t(xp, g, tm=512)
    out = peer_dense(xnt, xp, sel, u_b, vt_b, tn=tn, ec=ec)
    return out[:n] if pad else out


def _lam_init(layer):
    return 0.8 - 0.6 * math.exp(-0.3 * layer)


def _alibi_slopes():
    return jnp.exp2(-8.0 * jnp.arange(1, N_AH + 1, dtype=F32) / N_AH)


def layer_prompt(x, l, p, w_in_b, kbd, u_b, vt_b, *, b, s):
    n = b * s
    y = rms_matmul(x, p['norm1_g'][l], w_in_b, exact=False, tm=1024, tn=1024)
    g2 = jnp.tile(p['qk_norm_g'][l], (1, 2))
    qb, kn, kb, vb, vf = qk_norm(y, g2, tm=512)
    ya = attn_prompt(qb, kb, vb, p['diff_lambda'][l], p['subln_g'][l], _alibi_slopes(),
                     b=b, s=s, lam_init=_lam_init(l), tq=1024)
    yl, conv_new, h_last = lru_prompt(y, p['conv_w'][l], p['conv_b'][l], p['rg_wa'][l], p['rg_ba'][l],
                                      p['rg_wx'][l], p['rg_bx'][l], p['rg_lambda'][l], b=b, s=s, tt=256)
    yh, st = hgrn_prompt(y, p['hgrn_lb'], p['hgrn_norm_g'][l], b=b, s=s, layer=l)
    x = merge_out(x, y, ya, yl, yh, p['w_br_attn'][l], p['w_br_lru'][l], p['w_br_hgrn'][l],
                  p['w_out'][l], exact=False, tm=512)
    x = peer_ffn(x, p['norm2_g'][l], p['peer_wq'][l], kbd, u_b, vt_b, tn_sel=512, tn=512, ec=1024)
    k_out = kn.reshape(b, s, N_AH, 2 * HD)
    v_out = vf.reshape(b, s, N_AH, VD)
    s_out = jnp.swapaxes(st, -1, -2)
    return x, (k_out, v_out, conv_new, h_last.reshape(b, LRU_W), s_out)


DEC_PPS = 16
DEC_TOK = 8


def _swap_halves(a):
    return pltpu.roll(a, HD, a.ndim - 1)


def _attn_decode_kernel(pt_ref, q_ref, kn_ref, vn_ref, slope_ref, lv_ref, sg_ref, *refs,
                        pps, past, lam_init):
    k_refs = refs[:pps]
    v_refs = refs[pps:2 * pps]
    o_ref, m_ref, l_ref, acc_ref, s_ref = refs[2 * pps:]
    step = pl.program_id(1)

    @pl.when(step == 0)
    def _():
        m_ref[...] = jnp.full(m_ref.shape, NEG, F32)
        l_ref[...] = jnp.zeros_like(l_ref)
        acc_ref[...] = jnp.zeros_like(acc_ref)

    q = q_ref[...]
    seg = _seg_matrix()
    slope = slope_ref[...]
    tbias = slope[None] * lax.broadcasted_iota(jnp.int32, (PAGE_SIZE, N_AH, LANES), 0).astype(F32)

    def rescale(m_new):
        alpha = jnp.exp(m_ref[...] - m_new)
        l_ref[...] = l_ref[...] * alpha
        acc_ref[0] = acc_ref[0] * alpha
        acc_ref[1] = acc_ref[1] * _swap_halves(alpha)
        m_ref[...] = m_new

    offs = []
    m_new = m_ref[...]
    for i in range(pps):
        prod = (k_refs[i][...] * q[None]).reshape(PAGE_SIZE * N_AH, LANES).astype(BF16)
        s = _dot(prod, seg).reshape(PAGE_SIZE, N_AH, LANES) + tbias
        s_ref[i] = s
        offs.append(slope * ((step * pps + i) * PAGE_SIZE - past).astype(F32))
        m_new = jnp.maximum(m_new, jnp.max(s, axis=0) + offs[i])
    rescale(m_new)
    shift = [m_new - off for off in offs]

    def tokens_body(g, carry):
        l_a, a_s, a_x = carry
        rows = pl.ds(pl.multiple_of(g * DEC_TOK, DEC_TOK), DEC_TOK)
        for i in range(pps):
            pw = jnp.exp(s_ref[i, rows] - shift[i])
            vals = v_refs[i][rows]
            l_a = l_a + jnp.sum(pw, axis=0)
            a_s = a_s + jnp.sum(pw * vals, axis=0)
            a_x = a_x + jnp.sum(_swap_halves(pw) * vals, axis=0)
        return l_a, a_s, a_x

    zero = jnp.zeros((N_AH, LANES), F32)
    l_a, a_s, a_x = lax.fori_loop(0, PAGE_SIZE // DEC_TOK, tokens_body, (zero, zero, zero), unroll=2)
    l_ref[...] += l_a
    acc_ref[0] += a_s
    acc_ref[1] += a_x

    @pl.when(step == pl.num_programs(1) - 1)
    def _():
        sn = _dot2_exact_rhs(kn_ref[...] * q, seg)
        rescale(jnp.maximum(m_ref[...], sn))
        pn = jnp.exp(sn - m_ref[...])
        vn = vn_ref[...]
        l_ref[...] += pn
        acc_ref[0] += pn * vn
        acc_ref[1] += _swap_halves(pn) * vn
        first = lax.broadcasted_iota(jnp.int32, (N_AH, LANES), 1) < HD
        l = l_ref[...]
        l_x = _swap_halves(l)
        a_s = acc_ref[0]
        a_x = acc_ref[1]
        o0 = jnp.where(first, a_s, a_x) / jnp.where(first, l, l_x)
        o1 = jnp.where(first, a_x, a_s) / jnp.where(first, l_x, l)
        lam = _diff_lambda(lv_ref[...], lam_init)
        o = o0 - lam * o1
        o_ref[...] = _rms(o, sg_ref[...]) * (1.0 - lam_init)


def attn_decode(page_table, q, kn, vn, cache_k, cache_v, lv, sg, slopes, *, layer, lam_init):
    bd, n_pages = page_table.shape
    pps = DEC_PPS
    past = n_pages * PAGE_SIZE
    tok = lambda: pl.BlockSpec((None, N_AH, LANES), lambda b, st, pt: (b, 0, 0))
    page = lambda i: pl.BlockSpec((None, None, PAGE_SIZE, N_AH, LANES),
                                  lambda b, st, pt: (layer, pt[b, st * pps + i], 0, 0, 0))
    grid_spec = pltpu.PrefetchScalarGridSpec(
        num_scalar_prefetch=1,
        grid=(bd, n_pages // pps),
        in_specs=[
            tok(), tok(), tok(),
            pl.BlockSpec((N_AH, LANES), lambda b, st, pt: (0, 0)),
            pl.BlockSpec((4, HD), lambda b, st, pt: (0, 0)),
            pl.BlockSpec((1, VD), lambda b, st, pt: (0, 0)),
        ] + [page(i) for i in range(pps)] + [page(i) for i in range(pps)],
        out_specs=pl.BlockSpec((None, N_AH, LANES), lambda b, st, pt: (b, 0, 0)),
        scratch_shapes=[pltpu.VMEM((N_AH, LANES), F32), pltpu.VMEM((N_AH, LANES), F32),
                        pltpu.VMEM((2, N_AH, LANES), F32),
                        pltpu.VMEM((pps, PAGE_SIZE, N_AH, LANES), F32)],
    )
    slope_tile = jnp.broadcast_to(slopes[:, None], (N_AH, LANES))
    return pl.pallas_call(
        functools.partial(_attn_decode_kernel, pps=pps, past=past, lam_init=lam_init),
        grid_spec=grid_spec,
        out_shape=jax.ShapeDtypeStruct((bd, N_AH, LANES), F32),
        compiler_params=_cparams(("parallel", "arbitrary")),
        name="attn_decode",
    )(page_table, q, kn, vn, slope_tile, lv, sg.reshape(1, VD), *([cache_k] * pps), *([cache_v] * pps))


def _lru_step_kernel(lx_ref, lg_ref, buf_ref, h0_ref, cw_ref, cb_ref, wa_ref, ba_ref, wx_ref, bx_ref,
                     lam_ref, yl_ref, conv_ref, h_ref):
    x = lx_ref[...]
    cw = cw_ref[...]
    xc = cb_ref[...] + cw[CONV_W - 1:CONV_W, :] * x
    for j in range(CONV_W - 1):
        xc = xc + cw[j:j + 1, :] * buf_ref[j]
    r, i = _lru_gates(xc, wa_ref, ba_ref[...], wx_ref, bx_ref[...], True)
    a, u = _lru_au(xc, r, i, lam_ref[...])
    h = a * h0_ref[...] + u
    yl_ref[...] = h * _gelu(lg_ref[...])
    for j in range(CONV_W - 2):
        conv_ref[j] = buf_ref[j + 1]
    conv_ref[CONV_W - 2] = x
    h_ref[...] = h


def lru_step(y, buf_t, h0, cw, cb, wa, ba, wx, bx, lam):
    bd = y.shape[0]
    full = lambda shape: pl.BlockSpec(shape, lambda i: (0,) * len(shape))
    return pl.pallas_call(
        _lru_step_kernel,
        grid=(1,),
        in_specs=[
            pl.BlockSpec((bd, LRU_W), lambda i: (0, COL_LX)),
            pl.BlockSpec((bd, LRU_W), lambda i: (0, COL_LG)),
            full((CONV_W - 1, bd, LRU_W)), full((bd, LRU_W)), full((CONV_W, LRU_W)), full((1, LRU_W)),
            full((N_LB, LB, LB)), full((1, LRU_W)), full((N_LB, LB, LB)), full((1, LRU_W)), full((1, LRU_W)),
        ],
        out_specs=[full((bd, LRU_W)), full((CONV_W - 1, bd, LRU_W)), full((bd, LRU_W))],
        out_shape=[
            jax.ShapeDtypeStruct((bd, LRU_W), F32),
            jax.ShapeDtypeStruct((CONV_W - 1, bd, LRU_W), F32),
            jax.ShapeDtypeStruct((bd, LRU_W), F32),
        ],
        compiler_params=_cparams(("arbitrary",)),
        name="lru_step",
    )(y, y, buf_t, h0, cw, cb.reshape(1, -1), wa, ba.reshape(1, -1), wx, bx.reshape(1, -1), lam.reshape(1, -1))


def _hgrn_step_kernel(hq_ref, hf_ref, hi_ref, hg_ref, lb_ref, g_ref, s0_ref, y_ref, s1_ref, *, layer):
    lb = _hgrn_lower_bound([lb_ref[d] for d in range(DEPTH)], layer)
    z = hf_ref[...]
    f = jnp.exp(_hgrn_log_f(z, lb))
    kk = (1.0 - lb) * _sigmoid(-z)
    qq = _silu(hq_ref[...])
    vv = hi_ref[...]
    pad = jnp.zeros((HK - N_HH, LANES), F32)
    cols = lambda a: jnp.concatenate([a, pad], axis=0).T
    q_t, f_t, k_t = cols(qq), cols(f), cols(kk)
    row = lax.broadcasted_iota(jnp.int32, (N_HH, LANES), 0)
    o = jnp.zeros((N_HH, LANES), F32)
    for h in range(N_HH):
        s1 = s0_ref[h] * f_t[:, h:h + 1] + k_t[:, h:h + 1] * vv[h:h + 1, :]
        s1_ref[h] = s1
        oh = jnp.sum(q_t[:, h:h + 1] * s1, axis=0, keepdims=True)
        o = jnp.where(row == h, oh, o)
    y_ref[...] = _rms(o, g_ref[...]) * _silu(hg_ref[...])


def hgrn_step(y3, lb3, g, state, *, layer):
    bd = y3.shape[0]
    col = lambda c: pl.BlockSpec((None, N_HH, LANES), lambda b: (b, c, 0))
    return pl.pallas_call(
        functools.partial(_hgrn_step_kernel, layer=layer),
        grid=(bd,),
        in_specs=[
            col(COL_HQ), col(COL_HF), col(COL_HI), col(COL_HG),
            pl.BlockSpec((DEPTH, N_HH, LANES), lambda b: (0, 0, 0)),
            pl.BlockSpec((1, HV), lambda b: (0, 0)),
            pl.BlockSpec((None, None, N_HH, HK, HV), lambda b: (layer, b, 0, 0, 0)),
        ],
        out_specs=[
            pl.BlockSpec((None, N_HH, LANES), lambda b: (b, 0, 0)),
            pl.BlockSpec((None, N_HH, HK, HV), lambda b: (b, 0, 0, 0)),
        ],
        out_shape=[
            jax.ShapeDtypeStruct((bd, N_HH, LANES), F32),
            jax.ShapeDtypeStruct((bd, N_HH, HK, HV), F32),
        ],
        compiler_params=_cparams(("parallel",)),
        name="hgrn_step",
    )(y3, y3, y3, y3, lb3, g.reshape(1, HV), state)


def layer_sample(x, l, p, cache_k, cache_v, page_table, state_conv, state_lru, state_hgrn, kbd, u_b, vt_b):
    bd = x.shape[0]
    y = rms_matmul(x, p['norm1_g'][l], p['w_in'][l], exact=True, tm=bd, tn=1024)
    g2 = jnp.tile(p['qk_norm_g'][l], (1, 2))
    qf, kn, _, _, v_new = qk_norm(y, g2, tm=bd, q_dtype=F32)
    tok = lambda a: a.reshape(bd, N_AH, LANES)
    ya = attn_decode(page_table, tok(qf), tok(kn), tok(v_new), cache_k, cache_v, p['diff_lambda'][l],
                     p['subln_g'][l], _alibi_slopes(), layer=l, lam_init=_lam_init(l))
    yl, conv_t, h_new = lru_step(y, jnp.swapaxes(state_conv[l], 0, 1), state_lru[l], p['conv_w'][l],
                                 p['conv_b'][l], p['rg_wa'][l], p['rg_ba'][l], p['rg_wx'][l], p['rg_bx'][l],
                                 p['rg_lambda'][l])
    yh, s1 = hgrn_step(y.reshape(bd, IN_W // LANES, LANES), p['hgrn_lb'].reshape(DEPTH, N_HH, LANES),
                       p['hgrn_norm_g'][l], state_hgrn, layer=l)
    x = merge_out(x, y, ya.reshape(bd, ATTN_W), yl, yh.reshape(bd, HGRN_W), p['w_br_attn'][l],
                  p['w_br_lru'][l], p['w_br_hgrn'][l], p['w_out'][l], exact=True, tm=bd)
    x = peer_ffn(x, p['norm2_g'][l], p['peer_wq'][l], kbd, u_b, vt_b, tn_sel=512, tn=512, ec=1024)
    return x, (kn.reshape(bd, 1, N_AH, 2 * HD), v_new.reshape(bd, 1, N_AH, VD), jnp.swapaxes(conv_t, 0, 1),
               h_new, s1)


def kernel(x_prompt, x_sample, cache_k, cache_v, state_conv, state_lru, state_hgrn, page_table,
           norm1_g, norm2_g, w_in, qk_norm_g, diff_lambda, subln_g, conv_w, conv_b,
           rg_wa, rg_ba, rg_wx, rg_bx, rg_lambda, hgrn_lb, hgrn_norm_g,
           w_br_attn, w_br_lru, w_br_hgrn, w_out, peer_wq, peer_keys, peer_u, peer_v):
    p = dict(norm1_g=norm1_g, norm2_g=norm2_g, w_in=w_in, qk_norm_g=qk_norm_g, diff_lambda=diff_lambda,
             subln_g=subln_g, conv_w=conv_w, conv_b=conv_b, rg_wa=rg_wa, rg_ba=rg_ba, rg_wx=rg_wx,
             rg_bx=rg_bx, rg_lambda=rg_lambda, hgrn_lb=hgrn_lb, hgrn_norm_g=hgrn_norm_g,
             w_br_attn=w_br_attn, w_br_lru=w_br_lru, w_br_hgrn=w_br_hgrn, w_out=w_out,
             peer_wq=peer_wq, peer_keys=peer_keys, peer_u=peer_u, peer_v=peer_v)
    b, s, _ = x_prompt.shape
    bd = x_sample.shape[0]
    xp = x_prompt.reshape(b * s, D_MODEL)
    xs = x_sample.reshape(bd, D_MODEL)
    st_p, st_s = [], []
    for l in range(DEPTH):
        w_in_b = w_in[l].astype(BF16)
        kbd = _peer_key_blocks(peer_keys[l])
        u_b = peer_u[l].astype(BF16)
        vt_b = peer_v[l].T.astype(BF16)
        xp, sp = layer_prompt(xp, l, p, w_in_b, kbd, u_b, vt_b, b=b, s=s)
        xs, ss = layer_sample(xs, l, p, cache_k, cache_v, page_table, state_conv, state_lru, state_hgrn,
                              kbd, u_b, vt_b)
        st_p.append(sp)
        st_s.append(ss)
    k_p, v_p, conv_p, lru_p, hgrn_p = [jnp.stack(a) for a in zip(*st_p)]
    k_s, v_s, conv_s, lru_s, hgrn_s = [jnp.stack(a) for a in zip(*st_s)]
    return (xp.reshape(b, s, D_MODEL), xs.reshape(bd, 1, D_MODEL), k_p, v_p, conv_p, lru_p, hgrn_p,
            k_s, v_s, conv_s, lru_s, hgrn_s)
```
